```python
import math
import jax
import jax.numpy as jnp
from jax import lax
import numpy as np

D_MODEL = 1024
BATCH = 8
SEQ = 4096
DEPTH = 4

CHUNK = 64
Q_BLOCK = 128
EPS = 1e-6
MEM_LEN = 256

SSD_D_INNER = 1024
SSD_HEAD_DIM = 64
SSD_N_HEADS = SSD_D_INNER // SSD_HEAD_DIM
SSD_N_GROUPS = 4
SSD_HEADS_PER_GROUP = SSD_N_HEADS // SSD_N_GROUPS
SSD_D_STATE = 128
SSD_CONV = 4
SSD_XBC = SSD_D_INNER + 2 * SSD_N_GROUPS * SSD_D_STATE

CONV_D = 1024
CONV_K = 31

MLA_HEADS = 8
MLA_Q_RANK = 384
MLA_KV_RANK = 256
MLA_NOPE = 128
MLA_ROPE = 64
MLA_V = 128
MLA_QK = MLA_NOPE + MLA_ROPE
ROPE_THETA = 10000.0

N_BRANCH = 3
IN_SIZES = (SSD_D_INNER, SSD_XBC, SSD_N_HEADS, 2 * CONV_D, MLA_Q_RANK, MLA_KV_RANK + MLA_ROPE, N_BRANCH * D_MODEL)
IN_WIDTH = sum(IN_SIZES)

X_HEADS = 4
X_HEAD_DIM = D_MODEL // X_HEADS

FFN_HIDDEN = -(-(8 * D_MODEL) // (3 * 256)) * 256

kernel_name = 'hybrid_ssd_conformer_mla_block'


def split_cols(x, sizes):
    parts, start = [], 0
    for n in sizes:
        parts.append(x[..., start:start + n])
        start += n
    return parts


def rms_norm(x, g):
    xf = x.astype(jnp.float32)
    y = xf * lax.rsqrt(jnp.mean(xf * xf, axis=-1, keepdims=True) + EPS)
    return (y * g.astype(jnp.float32)).astype(x.dtype)


def layer_norm(x, g, b):
    xf = x.astype(jnp.float32)
    mu = jnp.mean(xf, axis=-1, keepdims=True)
    xc = xf - mu
    var = jnp.mean(xc * xc, axis=-1, keepdims=True)
    y = xc * lax.rsqrt(var + EPS) * g.astype(jnp.float32) + b.astype(jnp.float32)
    return y.astype(x.dtype)


def causal_depthwise_conv(x, w, b):
    k, c = w.shape
    y = lax.conv_general_dilated(x, w[:, None, :].astype(x.dtype), window_strides=(1,), padding=[(k - 1, 0)],
                                 dimension_numbers=('NWC', 'WIO', 'NWC'), feature_group_count=c)
    return y + b


def rope_tables(positions, dim):
    inv = ROPE_THETA ** (-jnp.arange(0, dim, 2, dtype=jnp.float32) / dim)
    ang = positions.astype(jnp.float32)[..., None] * inv
    return jnp.cos(ang), jnp.sin(ang)


def apply_rope(x, cos, sin):
    x1, x2 = jnp.split(x.astype(jnp.float32), 2, axis=-1)
    return jnp.concatenate([x1 * cos - x2 * sin, x2 * cos + x1 * sin], axis=-1).astype(x.dtype)


def segsum(a):
    l = a.shape[-1]
    cs = jnp.cumsum(a, axis=-1)
    seg = cs[..., :, None] - cs[..., None, :]
    mask = jnp.tril(jnp.ones((l, l), dtype=bool))
    return jnp.where(mask, seg, -jnp.inf)


def ssd_chunked_scan(xh, dt, A, Bg, Cg):
    b, s, G, R, P = xh.shape
    N = Bg.shape[-1]
    nc = s // CHUNK
    X = (xh * dt[..., None]).reshape(b, nc, CHUNK, G, R, P)
    a = (dt * A).reshape(b, nc, CHUNK, G, R).transpose(0, 3, 4, 1, 2)
    Bc = Bg.reshape(b, nc, CHUNK, G, N)
    Cc = Cg.reshape(b, nc, CHUNK, G, N)
    a_cs = jnp.cumsum(a, axis=-1)
    decay = jnp.exp(segsum(a))
    cb = jnp.einsum('bclgn,bcsgn->bgcls', Cc, Bc)
    y_diag = jnp.einsum('bgrcls,bcsgrp->bclgrp', cb[:, :, None] * decay, X)
    decay_states = jnp.exp(a_cs[..., -1:] - a_cs)
    states = jnp.einsum('bclgn,bgrcl,bclgrp->bcgrpn', Bc, decay_states, X)
    chunk_decay = jnp.exp(a_cs[..., -1])

    def step(h, inp):
        st, dec = inp
        return h * dec[..., None, None] + st, h

    h0 = jnp.zeros((b, G, R, P, N), dtype=X.dtype)
    _, prev = lax.scan(step, h0, (states.transpose(1, 0, 2, 3, 4, 5), chunk_decay.transpose(3, 0, 1, 2)))
    y_off = jnp.einsum('bclgn,cbgrpn,bgrcl->bclgrp', Cc, prev, jnp.exp(a_cs))
    return (y_diag + y_off).reshape(b, s, G, R, P)


def ssd_mixer(z, xbc, dt_raw, conv_w, conv_b, dt_bias, a_log, d_skip, norm_g, w_proj):
    b, s, _ = z.shape
    G, R, P, N = SSD_N_GROUPS, SSD_HEADS_PER_GROUP, SSD_HEAD_DIM, SSD_D_STATE
    xbc = jax.nn.silu(causal_depthwise_conv(xbc, conv_w, conv_b))
    xs, bm, cm = split_cols(xbc, (SSD_D_INNER, G * N, G * N))
    xh = xs.astype(jnp.float32).reshape(b, s, G, R, P)
    Bg = bm.astype(jnp.float32).reshape(b, s, G, N)
    Cg = cm.astype(jnp.float32).reshape(b, s, G, N)
    dt = jax.nn.softplus(dt_raw.astype(jnp.float32) + dt_bias.astype(jnp.float32)).reshape(b, s, G, R)
    A = -jnp.exp(a_log.astype(jnp.float32)).reshape(G, R)
    y = ssd_chunked_scan(xh, dt, A, Bg, Cg) + d_skip.astype(jnp.float32).reshape(G, R)[..., None] * xh
    y = y.reshape(b, s, SSD_D_INNER) * jax.nn.silu(z.astype(jnp.float32))
    yg = y.reshape(b, s, G, SSD_D_INNER // G)
    yg = yg * lax.rsqrt(jnp.mean(yg * yg, axis=-1, keepdims=True) + EPS)
    y = yg.reshape(b, s, SSD_D_INNER) * norm_g.astype(jnp.float32)
    return y.astype(z.dtype) @ w_proj


def conformer_conv_module(glu_in, dw_w, dw_b, ln_g, ln_b, w_pw):
    a, g = jnp.split(glu_in, 2, axis=-1)
    v = a * jax.nn.sigmoid(g)
    v = causal_depthwise_conv(v, dw_w, dw_b)
    v = jax.nn.silu(layer_norm(v, ln_g, ln_b))
    return v @ w_pw


def block_causal_attention(q, k, v):
    b, s, h, dq = q.shape
    dv = v.shape[-1]
    nb = s // Q_BLOCK
    scale = dq ** -0.5
    kt = k.transpose(0, 2, 1, 3)
    vt = v.transpose(0, 2, 1, 3)
    k_chunk = jnp.arange(s) // CHUNK
    qb = q.reshape(b, nb, Q_BLOCK, h, dq).transpose(1, 0, 3, 2, 4)

    def one_block(args):
        qblk, i = args
        q_chunk = (i * Q_BLOCK + jnp.arange(Q_BLOCK)) // CHUNK
        sc = jnp.einsum('bhqd,bhkd->bhqk', qblk, kt, preferred_element_type=jnp.float32) * scale
        sc = jnp.where(k_chunk[None, :] <= q_chunk[:, None], sc, -jnp.inf)
        p = jax.nn.softmax(sc, axis=-1)
        return jnp.einsum('bhqk,bhkd->bhqd', p.astype(v.dtype), vt)

    o = lax.map(one_block, (qb, jnp.arange(nb)))
    return o.transpose(1, 0, 3, 2, 4).reshape(b, s, h, dv)


def mla_mixer(q_lat, kv_lat, cos, sin, q_a_g, w_q_b, kv_a_g, w_kv_b, q_norm_g, k_norm_g, w_o):
    b, s, _ = q_lat.shape
    q = (rms_norm(q_lat, q_a_g) @ w_q_b).reshape(b, s, MLA_HEADS, MLA_QK)
    c_kv, k_rope = kv_lat[..., :MLA_KV_RANK], kv_lat[..., MLA_KV_RANK:]
    kv = (rms_norm(c_kv, kv_a_g) @ w_kv_b).reshape(b, s, MLA_HEADS, MLA_NOPE + MLA_V)
    k_nope, v = kv[..., :MLA_NOPE], kv[..., MLA_NOPE:]
    q_nope = rms_norm(q[..., :MLA_NOPE], q_norm_g[:MLA_NOPE])
    q_rope = apply_rope(rms_norm(q[..., MLA_NOPE:], q_norm_g[MLA_NOPE:]), cos[:, :, None], sin[:, :, None])
    k_nope = rms_norm(k_nope, k_norm_g[:MLA_NOPE])
    k_rope = apply_rope(rms_norm(k_rope, k_norm_g[MLA_NOPE:]), cos, sin)
    qf = jnp.concatenate([q_nope, q_rope], axis=-1)
    kf = jnp.concatenate([k_nope, jnp.broadcast_to(k_rope[:, :, None], (b, s, MLA_HEADS, MLA_ROPE))], axis=-1)
    o = block_causal_attention(qf, kf, v)
    return o.reshape(b, s, MLA_HEADS * MLA_V) @ w_o


def memory_cross_attention(h, mem_n, w_q, w_kv, q_norm_g, k_norm_g, w_o):
    b, s, _ = h.shape
    m = mem_n.shape[1]
    q = rms_norm((h @ w_q).reshape(b, s, X_HEADS, X_HEAD_DIM), q_norm_g)
    kv = (mem_n @ w_kv).reshape(b, m, 2, X_HEADS, X_HEAD_DIM)
    k = rms_norm(kv[:, :, 0], k_norm_g)
    v = kv[:, :, 1]
    sc = jnp.einsum('bqhd,bkhd->bhqk', q, k, preferred_element_type=jnp.float32) * (X_HEAD_DIM ** -0.5)
    p = jax.nn.softmax(sc, axis=-1)
    o = jnp.einsum('bhqk,bkhd->bqhd', p.astype(v.dtype), v)
    return o.reshape(b, s, D_MODEL) @ w_o


def swiglu_ffn(h, w_in, w_out):
    gate, up = jnp.split(h @ w_in, 2, axis=-1)
    return (jax.nn.silu(gate) * up) @ w_out


def setup_inputs(seed: int = 0) -> dict:
    key = jax.random.key(seed)
    ks = iter(jax.random.split(key, 64))
    f32 = jnp.float32
    L = DEPTH

    def normal(shape, std):
        return jax.random.normal(next(ks), shape, f32) * std

    def dense(shape, fan_in, scale=1.0):
        return normal(shape, scale * fan_in ** -0.5)

    def gain(shape):
        return 1.0 + normal(shape, 0.02)

    def small(shape):
        return normal(shape, 0.02)

    out_scale = 0.5
    x = normal((BATCH, SEQ, D_MODEL), 1.0)
    mem = normal((BATCH, MEM_LEN, D_MODEL), 1.0)
    start = jax.random.randint(next(ks), (BATCH, 1), 0, 100000, dtype=jnp.int32)
    positions = start + jnp.arange(SEQ, dtype=jnp.int32)[None, :]
    dt0 = jnp.exp(jax.random.uniform(next(ks), (L, SSD_N_HEADS), f32, math.log(1e-3), math.log(1e-1)))
    ssd_dt_bias = dt0 + jnp.log(-jnp.expm1(-dt0))
    ssd_a_log = jnp.log(jax.random.uniform(next(ks), (L, SSD_N_HEADS), f32, 1.0, 16.0))
    return {
        'x': x,
        'mem': mem,
        'positions': positions,
        'mix_norm_g': gain((L, D_MODEL)),
        'w_in': dense((L, D_MODEL, IN_WIDTH), D_MODEL),
        'ssd_conv_w': dense((L, SSD_CONV, SSD_XBC), SSD_CONV),
        'ssd_conv_b': small((L, SSD_XBC)),
        'ssd_dt_bias': ssd_dt_bias,
        'ssd_a_log': ssd_a_log,
        'ssd_d': gain((L, SSD_N_HEADS)),
        'ssd_norm_g': gain((L, SSD_D_INNER)),
        'ssd_w_out': dense((L, SSD_D_INNER, D_MODEL), SSD_D_INNER),
        'conv_dw_w': dense((L, CONV_K, CONV_D), CONV_K),
        'conv_dw_b': small((L, CONV_D)),
        'conv_ln_g': gain((L, CONV_D)),
        'conv_ln_b': small((L, CONV_D)),
        'conv_w_out': dense((L, CONV_D, D_MODEL), CONV_D),
        'mla_q_a_g': gain((L, MLA_Q_RANK)),
        'mla_w_q_b': dense((L, MLA_Q_RANK, MLA_HEADS * MLA_QK), MLA_Q_RANK),
        'mla_kv_a_g': gain((L, MLA_KV_RANK)),
        'mla_w_kv_b': dense((L, MLA_KV_RANK, MLA_HEADS * (MLA_NOPE + MLA_V)), MLA_KV_RANK),
        'mla_q_norm_g': gain((L, MLA_QK)),
        'mla_k_norm_g': gain((L, MLA_QK)),
        'mla_w_o': dense((L, MLA_HEADS * MLA_V, D_MODEL), MLA_HEADS * MLA_V),
        'gate_b': small((L, N_BRANCH, D_MODEL)),
        'w_out': dense((L, D_MODEL, D_MODEL), D_MODEL, out_scale),
        'xattn_norm_g': gain((L, D_MODEL)),
        'mem_norm_g': gain((L, D_MODEL)),
        'xattn_w_q': dense((L, D_MODEL, D_MODEL), D_MODEL),
        'xattn_w_kv': dense((L, D_MODEL, 2 * D_MODEL), D_MODEL),
        'xattn_q_norm_g': gain((L, X_HEAD_DIM)),
        'xattn_k_norm_g': gain((L, X_HEAD_DIM)),
        'xattn_w_o': dense((L, D_MODEL, D_MODEL), D_MODEL, out_scale),
        'ffn_norm_g': gain((L, D_MODEL)),
        'ffn_w_in': dense((L, D_MODEL, 2 * FFN_HIDDEN), D_MODEL),
        'ffn_w_out': dense((L, FFN_HIDDEN, D_MODEL), FFN_HIDDEN, out_scale),
    }


def reference(x, mem, positions, mix_norm_g, w_in, ssd_conv_w, ssd_conv_b, ssd_dt_bias, ssd_a_log, ssd_d,
              ssd_norm_g, ssd_w_out, conv_dw_w, conv_dw_b, conv_ln_g, conv_ln_b, conv_w_out, mla_q_a_g,
              mla_w_q_b, mla_kv_a_g, mla_w_kv_b, mla_q_norm_g, mla_k_norm_g, mla_w_o, gate_b, w_out,
              xattn_norm_g, mem_norm_g, xattn_w_q, xattn_w_kv, xattn_q_norm_g, xattn_k_norm_g, xattn_w_o,
              ffn_norm_g, ffn_w_in, ffn_w_out):
    b, s, _ = x.shape
    cos, sin = rope_tables(positions, MLA_ROPE)
    for l in range(DEPTH):
        u = rms_norm(x, mix_norm_g[l])
        z, xbc, dt_raw, glu_in, q_lat, kv_lat, gate_logits = split_cols(u @ w_in[l], IN_SIZES)
        y_ssd = ssd_mixer(z, xbc, dt_raw, ssd_conv_w[l], ssd_conv_b[l], ssd_dt_bias[l], ssd_a_log[l],
                          ssd_d[l], ssd_norm_g[l], ssd_w_out[l])
        y_conv = conformer_conv_module(glu_in, conv_dw_w[l], conv_dw_b[l], conv_ln_g[l], conv_ln_b[l],
                                       conv_w_out[l])
        y_mla = mla_mixer(q_lat, kv_lat, cos, sin, mla_q_a_g[l], mla_w_q_b[l], mla_kv_a_g[l], mla_w_kv_b[l],
                          mla_q_norm_g[l], mla_k_norm_g[l], mla_w_o[l])
        gates = jax.nn.sigmoid((gate_logits + gate_b[l].reshape(-1)).astype(jnp.float32))
        gates = gates.astype(x.dtype).reshape(b, s, N_BRANCH, D_MODEL)
        merged = gates[:, :, 0] * y_ssd + gates[:, :, 1] * y_conv + gates[:, :, 2] * y_mla
        x = x + merged @ w_out[l]
        x = x + memory_cross_attention(rms_norm(x, xattn_norm_g[l]), rms_norm(mem, mem_norm_g[l]),
                                       xattn_w_q[l], xattn_w_kv[l], xattn_q_norm_g[l], xattn_k_norm_g[l],
                                       xattn_w_o[l])
        x = x + swiglu_ffn(rms_norm(x, ffn_norm_g[l]), ffn_w_in[l], ffn_w_out[l])
    return x
```

```python
import functools
import math

import jax
import jax.numpy as jnp
import numpy as np
from jax import lax
from jax.experimental import pallas as pl
from jax.experimental.pallas import tpu as pltpu

F32 = jnp.float32
BF16 = jnp.bfloat16

EPS = 1e-6
CHUNK = 64
D_MODEL = 1024

SSD_D_INNER = 1024
SSD_HEAD_DIM = 64
SSD_N_HEADS = 16
SSD_N_GROUPS = 4
SSD_HEADS_PER_GROUP = 4
SSD_D_STATE = 128
SSD_CONV = 4
SSD_XBC = 2048
SSD_GROUP_W = SSD_HEADS_PER_GROUP * SSD_HEAD_DIM

CONV_D = 1024
CONV_K = 31
CONV_HALO = 32

MLA_HEADS = 8
MLA_Q_RANK = 384
MLA_Q_RANK_PAD = 512
MLA_KV_RANK = 256
MLA_NOPE = 128
MLA_ROPE = 64
MLA_V = 128
MLA_QK = MLA_NOPE + MLA_ROPE
MLA_QK_PAD = 256
ROPE_THETA = 10000.0

X_HEADS = 4
X_HEAD_DIM = 256
FFN_HIDDEN = 2816

LANE = 128
SUBLANE = 8

OFF_XBC = 0
OFF_GLU_A = 2048
OFF_GLU_G = 3072
OFF_GATE = 4096
OFF_Z = 7168
OFF_QLAT = 8192
OFF_CKV = 8704
OFF_MISC = 8960
OFF_KROT = 9088
PROJ_W = 9216
MISC_DT = 64

VMEM_LIMIT = 56 * 1024 * 1024


def _cparams(sem):
    return pltpu.CompilerParams(dimension_semantics=sem, vmem_limit_bytes=VMEM_LIMIT)


def _dot(a, b):
    return jnp.dot(a, b, preferred_element_type=F32)


def _dot_nt(a, b):
    return lax.dot_general(a, b, (((1,), (1,)), ((), ())), preferred_element_type=F32)


def _split3(x):
    hi = x.astype(BF16)
    r = x - hi.astype(F32)
    mid = r.astype(BF16)
    lo = (r - mid.astype(F32)).astype(BF16)
    return hi, mid, lo


def _dot_exact_rhs01(x, m01):
    hi, mid, lo = _split3(x)
    return _dot(hi, m01) + _dot(mid, m01) + _dot(lo, m01)


def _dot_exact_lhs01(m01, x):
    hi, mid, lo = _split3(x)
    return _dot(m01, hi) + _dot(m01, mid) + _dot(m01, lo)


def _sigmoid(x):
    return 1.0 / (1.0 + jnp.exp(-x))


def _silu(x):
    return x * _sigmoid(x)


def _in_proj_kernel(x_ref, g_ref, w_ref, o_ref, u_ref):
    @pl.when(pl.program_id(1) == 0)
    def _():
        x = x_ref[...]
        ms = jnp.mean(x * x, axis=-1, keepdims=True)
        u_ref[...] = (x * lax.rsqrt(ms + EPS) * g_ref[...]).astype(BF16)

    o_ref[...] = _dot(u_ref[...], w_ref[...]).astype(o_ref.dtype)


def _in_proj(x2, g, w, tm, tn):
    t, d = x2.shape
    n = w.shape[1]
    return pl.pallas_call(
        _in_proj_kernel,
        grid=(t // tm, n // tn),
        in_specs=[
            pl.BlockSpec((tm, d), lambda i, j: (i, 0)),
            pl.BlockSpec((1, d), lambda i, j: (0, 0)),
            pl.BlockSpec((d, tn), lambda i, j: (0, j)),
        ],
        out_specs=pl.BlockSpec((tm, tn), lambda i, j: (i, j)),
        out_shape=jax.ShapeDtypeStruct((t, n), F32),
        scratch_shapes=[pltpu.VMEM((tm, d), BF16)],
        compiler_params=_cparams(("parallel", "arbitrary")),
        name="in_proj",
    )(x2, g, w)


def _ssd_kernel(xbc_ref, z_ref, misc_ref, convw_ref, convb_ref, dtb_ref, alog_s_ref, alog_e_ref,
                dskip_ref, normg_ref, expand_ref, tri_ref, y_ref, xpad_ref, state_ref, *, L, n_chunks):
    ts = L * n_chunks
    pad = SUBLANE

    @pl.when(pl.program_id(1) == 0)
    def _():
        xpad_ref[0:pad, :] = jnp.zeros((pad, SSD_XBC), F32)
        state_ref[...] = jnp.zeros(state_ref.shape, F32)

    xpad_ref[pad:pad + ts, :] = xbc_ref[0]
    conv = jnp.broadcast_to(convb_ref[...], (ts, SSD_XBC))
    for k in range(SSD_CONV):
        conv = conv + convw_ref[k:k + 1, :] * xpad_ref[pl.ds(pad - (SSD_CONV - 1) + k, ts), :]
    xpad_ref[0:pad, :] = xpad_ref[ts:ts + pad, :]
    xc = _silu(conv)

    misc = misc_ref[0]
    v = misc + dtb_ref[...]
    dt_s = jnp.maximum(v, 0.0) + jnp.log1p(jnp.exp(-jnp.abs(v)))
    a_neg_s = -jnp.exp(alog_s_ref[...])
    a_neg_e = -jnp.exp(alog_e_ref[...])
    expand = expand_ref[...]
    tri = tri_ref[...]
    dt_e = _dot_exact_rhs01(dt_s, expand)
    a_s = dt_s * a_neg_s
    a_e = dt_e * a_neg_e

    row_i = lax.broadcasted_iota(jnp.int32, (L, L), 0)
    col_i = lax.broadcasted_iota(jnp.int32, (L, L), 1)
    causal = row_i >= col_i
    lane_g = lax.broadcasted_iota(jnp.int32, (1, SSD_GROUP_W), 1) // SSD_HEAD_DIM

    z = z_ref[0]
    for c in range(n_chunks):
        r0 = c * L
        xs = xc[r0:r0 + L, 0:SSD_D_INNER]
        acs_s = _dot_exact_lhs01(tri, a_s[r0:r0 + L])
        acs_e = _dot_exact_lhs01(tri, a_e[r0:r0 + L])
        acs_st = acs_s.T
        last_e = acs_e[L - 1:L, :]
        exp_acs = jnp.exp(acs_e)
        to_end = jnp.exp(last_e - acs_e)
        chunk_decay = jnp.exp(last_e)
        xdt = xs * dt_e[r0:r0 + L]
        xdt_b = xdt.astype(BF16)
        xend_b = (xdt * to_end).astype(BF16)
        y_groups = []
        for g in range(SSD_N_GROUPS):
            gs = slice(g * SSD_GROUP_W, (g + 1) * SSD_GROUP_W)
            b_g = xc[r0:r0 + L, SSD_D_INNER + g * SSD_D_STATE:SSD_D_INNER + (g + 1) * SSD_D_STATE]
            c_g = xc[r0:r0 + L, SSD_D_INNER + (SSD_N_GROUPS + g) * SSD_D_STATE:
                     SSD_D_INNER + (SSD_N_GROUPS + g + 1) * SSD_D_STATE]
            b_gt = b_g.T.astype(BF16)
            c_gb = c_g.astype(BF16)
            cb = _dot(c_gb, b_gt)
            st = state_ref[g]
            y_g = _dot(c_gb, st.astype(BF16)) * exp_acs[:, gs]
            x_g = xdt_b[:, gs]
            for r in range(SSD_HEADS_PER_GROUP):
                h = g * SSD_HEADS_PER_GROUP + r
                col = acs_s[:, MISC_DT + h:MISC_DT + h + 1]
                row = acs_st[MISC_DT + h:MISC_DT + h + 1, :]
                decay = jnp.exp(jnp.where(causal, col - row, -jnp.inf))
                gm = (cb * decay).astype(BF16)
                x_h = jnp.where(lane_g == r, x_g, jnp.zeros_like(x_g))
                y_g = y_g + _dot(gm, x_h)
            state_ref[g] = st * chunk_decay[:, gs] + _dot(b_gt, xend_b[:, gs])
            y_groups.append(y_g)
        y = jnp.concatenate(y_groups, axis=-1) + dskip_ref[...] * xs
        y = y * _silu(z[r0:r0 + L])
        outs = []
        for g in range(SSD_N_GROUPS):
            yg = y[:, g * SSD_GROUP_W:(g + 1) * SSD_GROUP_W]
            outs.append(yg * lax.rsqrt(jnp.mean(yg * yg, axis=-1, keepdims=True) + EPS))
        y = jnp.concatenate(outs, axis=-1) * normg_ref[...]
        y_ref[0, r0:r0 + L, :] = y.astype(y_ref.dtype)


def _ssd(proj3, convw, convb, dtb, alog_s, alog_e, dskip_e, normg, expand, tri, L, n_chunks):
    b, s, _ = proj3.shape
    ts = L * n_chunks
    full = lambda shape: pl.BlockSpec(shape, lambda bi, si: (0,) * len(shape))
    return pl.pallas_call(
        functools.partial(_ssd_kernel, L=L, n_chunks=n_chunks),
        grid=(b, s // ts),
        in_specs=[
            pl.BlockSpec((1, ts, SSD_XBC), lambda bi, si: (bi, si, OFF_XBC // SSD_XBC)),
            pl.BlockSpec((1, ts, SSD_D_INNER), lambda bi, si: (bi, si, OFF_Z // SSD_D_INNER)),
            pl.BlockSpec((1, ts, LANE), lambda bi, si: (bi, si, OFF_MISC // LANE)),
            full((SSD_CONV, SSD_XBC)), full((1, SSD_XBC)), full((1, LANE)), full((1, LANE)),
            full((1, SSD_D_INNER)), full((1, SSD_D_INNER)), full((1, SSD_D_INNER)),
            full((LANE, SSD_D_INNER)), full((L, L)),
        ],
        out_specs=pl.BlockSpec((1, ts, SSD_D_INNER), lambda bi, si: (bi, si, 0)),
        out_shape=jax.ShapeDtypeStruct((b, s, SSD_D_INNER), BF16),
        scratch_shapes=[
            pltpu.VMEM((SUBLANE + ts, SSD_XBC), F32),
            pltpu.VMEM((SSD_N_GROUPS, SSD_D_STATE, SSD_GROUP_W), F32),
        ],
        compiler_params=_cparams(("parallel", "arbitrary")),
        name="ssd",
    )(proj3, proj3, proj3, convw, convb, dtb, alog_s, alog_e, dskip_e, normg, expand, tri)


def _conv_kernel(a_ref, g_ref, w_ref, b_ref, lng_ref, lnb_ref, o_ref, vpad_ref, acc_ref, *, tc, rb):
    @pl.when(pl.program_id(1) == 0)
    def _():
        vpad_ref[0:CONV_HALO, :] = jnp.zeros((CONV_HALO, CONV_D), F32)

    vpad_ref[CONV_HALO:CONV_HALO + tc, :] = a_ref[0] * _sigmoid(g_ref[0])
    first = CONV_HALO - (CONV_K - 1)
    for r0 in range(0, tc, rb):
        acc = jnp.broadcast_to(b_ref[...], (rb, CONV_D))
        for j in range(CONV_K):
            acc = acc + w_ref[j:j + 1, :] * vpad_ref[pl.ds(first + j + r0, rb), :]
        acc_ref[r0:r0 + rb, :] = acc
    vpad_ref[0:CONV_HALO, :] = vpad_ref[tc:tc + CONV_HALO, :]
    v = acc_ref[...]
    mu = jnp.mean(v, axis=-1, keepdims=True)
    vc = v - mu
    var = jnp.mean(vc * vc, axis=-1, keepdims=True)
    y = vc * lax.rsqrt(var + EPS) * lng_ref[...] + lnb_ref[...]
    o_ref[0] = _silu(y).astype(o_ref.dtype)


def _conformer_conv(proj3, w, bias, lng, lnb, tc, rb):
    b, s, _ = proj3.shape
    full = lambda shape: pl.BlockSpec(shape, lambda bi, si: (0,) * len(shape))
    return pl.pallas_call(
        functools.partial(_conv_kernel, tc=tc, rb=rb),
        grid=(b, s // tc),
        in_specs=[
            pl.BlockSpec((1, tc, CONV_D), lambda bi, si: (bi, si, OFF_GLU_A // CONV_D)),
            pl.BlockSpec((1, tc, CONV_D), lambda bi, si: (bi, si, OFF_GLU_G // CONV_D)),
            full((CONV_K, CONV_D)), full((1, CONV_D)), full((1, CONV_D)), full((1, CONV_D)),
        ],
        out_specs=pl.BlockSpec((1, tc, CONV_D), lambda bi, si: (bi, si, 0)),
        out_shape=jax.ShapeDtypeStruct((b, s, CONV_D), BF16),
        scratch_shapes=[pltpu.VMEM((CONV_HALO + tc, CONV_D), F32), pltpu.VMEM((tc, CONV_D), F32)],
        compiler_params=_cparams(("parallel", "arbitrary")),
        name="conformer_conv",
    )(proj3, proj3, w, bias, lng, lnb)


def _mla_prep_kernel(qlat_ref, ckv_ref, misc_ref, krot_ref, cos_ref, sin_ref, qag_ref, wq_ref, kvag_ref,
                     wkv_ref, qg_nope_ref, qg_rope_ref, qg_rot_ref, kg_nope_ref, kg_rope_ref, kg_rot_ref,
                     q_ref, k_ref, v_ref):
    hn = MLA_HEADS * MLA_NOPE
    cos2 = cos_ref[0]
    sin2 = sin_ref[0]
    scale = MLA_QK ** -0.5

    ql = qlat_ref[0]
    ms = jnp.sum(ql * ql, axis=-1, keepdims=True) * (1.0 / MLA_Q_RANK)
    qa = (ql * lax.rsqrt(ms + EPS) * qag_ref[...]).astype(BF16)
    qf = _dot(qa, wq_ref[...])
    for h in range(MLA_HEADS):
        qn = qf[:, h * LANE:(h + 1) * LANE]
        qn = qn * lax.rsqrt(jnp.mean(qn * qn, axis=-1, keepdims=True) + EPS) * qg_nope_ref[...]
        qr = qf[:, hn + h * LANE:hn + (h + 1) * LANE]
        qs = qf[:, 2 * hn + h * LANE:2 * hn + (h + 1) * LANE]
        inv = lax.rsqrt(jnp.sum(qr * qr, axis=-1, keepdims=True) * (1.0 / MLA_ROPE) + EPS)
        qro = (qr * qg_rope_ref[...] * cos2 + qs * qg_rot_ref[...] * sin2) * inv
        q_ref[0, :, h * MLA_QK_PAD:h * MLA_QK_PAD + LANE] = (qn * scale).astype(q_ref.dtype)
        q_ref[0, :, h * MLA_QK_PAD + LANE:(h + 1) * MLA_QK_PAD] = (qro * scale).astype(q_ref.dtype)

    ckv = ckv_ref[0]
    ca = (ckv * lax.rsqrt(jnp.mean(ckv * ckv, axis=-1, keepdims=True) + EPS) * kvag_ref[...]).astype(BF16)
    kvf = _dot(ca, wkv_ref[...])
    lane = lax.broadcasted_iota(jnp.int32, (1, LANE), 1)
    kr = jnp.where(lane < MLA_ROPE, misc_ref[0], 0.0)
    ks = krot_ref[0]
    inv = lax.rsqrt(jnp.sum(kr * kr, axis=-1, keepdims=True) * (1.0 / MLA_ROPE) + EPS)
    kro = ((kr * kg_rope_ref[...] * cos2 + ks * kg_rot_ref[...] * sin2) * inv).astype(k_ref.dtype)
    for h in range(MLA_HEADS):
        kn = kvf[:, h * LANE:(h + 1) * LANE]
        kn = kn * lax.rsqrt(jnp.mean(kn * kn, axis=-1, keepdims=True) + EPS) * kg_nope_ref[...]
        k_ref[0, :, h * MLA_QK_PAD:h * MLA_QK_PAD + LANE] = kn.astype(k_ref.dtype)
        k_ref[0, :, h * MLA_QK_PAD + LANE:(h + 1) * MLA_QK_PAD] = kro
    v_ref[0] = kvf[:, hn:].astype(v_ref.dtype)


def _mla_prep(proj3, cos2, sin2, qag, wq, kvag, wkv, qg_nope, qg_rope, qg_rot, kg_nope, kg_rope, kg_rot, tm):
    b, s, _ = proj3.shape
    full = lambda shape: pl.BlockSpec(shape, lambda bi, si: (0,) * len(shape))
    hq = MLA_HEADS * MLA_QK_PAD
    hv = MLA_HEADS * MLA_V
    return pl.pallas_call(
        _mla_prep_kernel,
        grid=(b, s // tm),
        in_specs=[
            pl.BlockSpec((1, tm, MLA_Q_RANK_PAD), lambda bi, si: (bi, si, OFF_QLAT // MLA_Q_RANK_PAD)),
            pl.BlockSpec((1, tm, MLA_KV_RANK), lambda bi, si: (bi, si, OFF_CKV // MLA_KV_RANK)),
            pl.BlockSpec((1, tm, LANE), lambda bi, si: (bi, si, OFF_MISC // LANE)),
            pl.BlockSpec((1, tm, LANE), lambda bi, si: (bi, si, OFF_KROT // LANE)),
            pl.BlockSpec((1, tm, LANE), lambda bi, si: (bi, si, 0)),
            pl.BlockSpec((1, tm, LANE), lambda bi, si: (bi, si, 0)),
            full((1, MLA_Q_RANK_PAD)), full(wq.shape), full((1, MLA_KV_RANK)), full(wkv.shape),
            full((1, LANE)), full((1, LANE)), full((1, LANE)), full((1, LANE)), full((1, LANE)), full((1, LANE)),
        ],
        out_specs=[
            pl.BlockSpec((1, tm, hq), lambda bi, si: (bi, si, 0)),
            pl.BlockSpec((1, tm, hq), lambda bi, si: (bi, si, 0)),
            pl.BlockSpec((1, tm, hv), lambda bi, si: (bi, si, 0)),
        ],
        out_shape=[
            jax.ShapeDtypeStruct((b, s, hq), BF16),
            jax.ShapeDtypeStruct((b, s, hq), BF16),
            jax.ShapeDtypeStruct((b, s, hv), BF16),
        ],
        compiler_params=_cparams(("parallel", "parallel")),
        name="mla_prep",
    )(proj3, proj3, proj3, proj3, cos2, sin2, qag, wq, kvag, wkv, qg_nope, qg_rope, qg_rot, kg_nope, kg_rope,
      kg_rot)


def _attn_kernel(q_ref, k_ref, v_ref, o_ref, *, tq):
    i = pl.program_id(2)
    q = q_ref[0]

    def block(j, carry, masked):
        m, l, acc = carry
        start = pl.multiple_of(j * tq, tq)
        k = k_ref[0, pl.ds(start, tq), :]
        v = v_ref[0, pl.ds(start, tq), :]
        s = _dot_nt(q, k)
        if masked:
            rq = lax.broadcasted_iota(jnp.int32, (tq, tq), 0) // CHUNK
            ck = lax.broadcasted_iota(jnp.int32, (tq, tq), 1) // CHUNK
            s = jnp.where(ck <= rq, s, -jnp.inf)
        m_new = jnp.maximum(m, jnp.max(s, axis=-1, keepdims=True))
        alpha = jnp.exp(m - m_new)
        p = jnp.exp(s - m_new)
        l = alpha * l + jnp.sum(p, axis=-1, keepdims=True)
        acc = alpha * acc + _dot(p.astype(BF16), v)
        return m_new, l, acc

    init = (jnp.full((tq, 1), -jnp.inf, F32), jnp.zeros((tq, 1), F32), jnp.zeros((tq, MLA_V), F32))
    carry = lax.fori_loop(0, i, lambda j, c: block(j, c, False), init)
    _, l, acc = block(i, carry, True)
    o_ref[0] = (acc / l).astype(o_ref.dtype)


def _attention(q, k, v, tq):
    b, s, _ = q.shape
    return pl.pallas_call(
        functools.partial(_attn_kernel, tq=tq),
        grid=(b, MLA_HEADS, s // tq),
        in_specs=[
            pl.BlockSpec((1, tq, MLA_QK_PAD), lambda bi, h, i: (bi, i, h)),
            pl.BlockSpec((1, s, MLA_QK_PAD), lambda bi, h, i: (bi, 0, h)),
            pl.BlockSpec((1, s, MLA_V), lambda bi, h, i: (bi, 0, h)),
        ],
        out_specs=pl.BlockSpec((1, tq, MLA_V), lambda bi, h, i: (bi, i, h)),
        out_shape=jax.ShapeDtypeStruct((b, s, MLA_HEADS * MLA_V), BF16),
        compiler_params=_cparams(("parallel", "parallel", "arbitrary")),
        name="mla_attention",
    )(q, k, v)


def _merge_kernel(x_ref, ssd_ref, conv_ref, att_ref, g0_ref, g1_ref, g2_ref, gb_ref, w_ssd_ref, w_conv_ref,
                  w_mla_ref, w_out_ref, o_ref):
    merged = _sigmoid(g0_ref[...] + gb_ref[0:1, :]) * _dot(ssd_ref[...], w_ssd_ref[...])
    merged = merged + _sigmoid(g1_ref[...] + gb_ref[1:2, :]) * _dot(conv_ref[...], w_conv_ref[...])
    merged = merged + _sigmoid(g2_ref[...] + gb_ref[2:3, :]) * _dot(att_ref[...], w_mla_ref[...])
    o_ref[...] = x_ref[...] + _dot(merged.astype(BF16), w_out_ref[...])


def _merge(x2, ssd_pre, conv_pre, att, proj2, gate_b, w_ssd, w_conv, w_mla, w_out, tm):
    t, d = x2.shape
    row = lambda c: pl.BlockSpec((tm, d), lambda i, c=c: (i, c))
    full = lambda shape: pl.BlockSpec(shape, lambda i: (0,) * len(shape))
    g0 = OFF_GATE // d
    return pl.pallas_call(
        _merge_kernel,
        grid=(t // tm,),
        in_specs=[row(0), row(0), row(0), row(0), row(g0), row(g0 + 1), row(g0 + 2), full(gate_b.shape),
                  full((d, d)), full((d, d)), full((d, d)), full((d, d))],
        out_specs=row(0),
        out_shape=jax.ShapeDtypeStruct((t, d), F32),
        compiler_params=_cparams(("parallel",)),
        name="merge",
    )(x2, ssd_pre, conv_pre, att, proj2, proj2, proj2, gate_b, w_ssd, w_conv, w_mla, w_out)


def _mem_kv_kernel(mem_ref, g_ref, w_ref, kg_ref, k_ref, v_ref):
    m = mem_ref[0]
    mn = (m * lax.rsqrt(jnp.mean(m * m, axis=-1, keepdims=True) + EPS) * g_ref[...]).astype(BF16)
    kv = _dot(mn, w_ref[...])
    for h in range(X_HEADS):
        kh = kv[:, h * X_HEAD_DIM:(h + 1) * X_HEAD_DIM]
        kh = kh * lax.rsqrt(jnp.mean(kh * kh, axis=-1, keepdims=True) + EPS) * kg_ref[...]
        k_ref[0, :, h * X_HEAD_DIM:(h + 1) * X_HEAD_DIM] = kh.astype(k_ref.dtype)
    v_ref[0] = kv[:, D_MODEL:].astype(v_ref.dtype)


def _mem_kv(mem, g, w_kv, kg):
    b, m, d = mem.shape
    full = lambda shape: pl.BlockSpec(shape, lambda bi: (0,) * len(shape))
    blk = pl.BlockSpec((1, m, d), lambda bi: (bi, 0, 0))
    return pl.pallas_call(
        _mem_kv_kernel,
        grid=(b,),
        in_specs=[blk, full((1, d)), full(w_kv.shape), full((1, X_HEAD_DIM))],
        out_specs=[blk, blk],
        out_shape=[jax.ShapeDtypeStruct((b, m, d), BF16), jax.ShapeDtypeStruct((b, m, d), BF16)],
        compiler_params=_cparams(("parallel",)),
        name="mem_kv",
    )(mem, g, w_kv, kg)


def _xattn_kernel(x_ref, k_ref, v_ref, g_ref, wq_ref, qg_ref, wo_ref, o_ref):
    x = x_ref[0]
    u = (x * lax.rsqrt(jnp.mean(x * x, axis=-1, keepdims=True) + EPS) * g_ref[...]).astype(BF16)
    q = _dot(u, wq_ref[...])
    scale = X_HEAD_DIM ** -0.5
    outs = []
    for h in range(X_HEADS):
        hs = slice(h * X_HEAD_DIM, (h + 1) * X_HEAD_DIM)
        qh = q[:, hs]
        qh = qh * lax.rsqrt(jnp.mean(qh * qh, axis=-1, keepdims=True) + EPS) * (qg_ref[...] * scale)
        s = _dot_nt(qh.astype(BF16), k_ref[0, :, hs])
        p = jnp.exp(s - jnp.max(s, axis=-1, keepdims=True))
        o = _dot(p.astype(BF16), v_ref[0, :, hs])
        outs.append((o / jnp.sum(p, axis=-1, keepdims=True)).astype(BF16))
    o_ref[0] = x + _dot(jnp.concatenate(outs, axis=-1), wo_ref[...])


def _xattn(x3, k, v, g, wq, qg, wo, tm):
    b, s, d = x3.shape
    m = k.shape[1]
    full = lambda shape: pl.BlockSpec(shape, lambda bi, si: (0,) * len(shape))
    return pl.pallas_call(
        _xattn_kernel,
        grid=(b, s // tm),
        in_specs=[
            pl.BlockSpec((1, tm, d), lambda bi, si: (bi, si, 0)),
            pl.BlockSpec((1, m, d), lambda bi, si: (bi, 0, 0)),
            pl.BlockSpec((1, m, d), lambda bi, si: (bi, 0, 0)),
            full((1, d)), full((d, d)), full((1, X_HEAD_DIM)), full((d, d)),
        ],
        out_specs=pl.BlockSpec((1, tm, d), lambda bi, si: (bi, si, 0)),
        out_shape=jax.ShapeDtypeStruct((b, s, d), F32),
        compiler_params=_cparams(("parallel", "parallel")),
        name="mem_xattn",
    )(x3, k, v, g, wq, qg, wo)


def _ffn_kernel(x_ref, g_ref, wg_ref, wu_ref, wo_ref, o_ref, *, hc):
    x = x_ref[...]
    u = (x * lax.rsqrt(jnp.mean(x * x, axis=-1, keepdims=True) + EPS) * g_ref[...]).astype(BF16)
    acc = x
    for c in range(0, FFN_HIDDEN, hc):
        gate = _dot(u, wg_ref[:, c:c + hc])
        up = _dot(u, wu_ref[:, c:c + hc])
        acc = acc + _dot((_silu(gate) * up).astype(BF16), wo_ref[c:c + hc, :])
    o_ref[...] = acc


def _ffn(x2, g, wg, wu, wo, tm, hc):
    t, d = x2.shape
    full = lambda shape: pl.BlockSpec(shape, lambda i: (0,) * len(shape))
    return pl.pallas_call(
        functools.partial(_ffn_kernel, hc=hc),
        grid=(t // tm,),
        in_specs=[pl.BlockSpec((tm, d), lambda i: (i, 0)), full((1, d)), full(wg.shape), full(wu.shape),
                  full(wo.shape)],
        out_specs=pl.BlockSpec((tm, d), lambda i: (i, 0)),
        out_shape=jax.ShapeDtypeStruct((t, d), F32),
        compiler_params=_cparams(("parallel",)),
        name="ffn",
    )(x2, g, wg, wu, wo)


def _swap_halves(a):
    h = a.shape[-1] // 2
    return jnp.concatenate([a[..., h:], a[..., :h]], axis=-1)


def _pad_last(a, width):
    return jnp.pad(a, [(0, 0)] * (a.ndim - 1) + [(0, width - a.shape[-1])])


def _in_proj_weight(w):
    z, xbc, dt, glu, q_lat, kv_lat, gates = jnp.split(
        w, np.cumsum([SSD_D_INNER, SSD_XBC, SSD_N_HEADS, 2 * CONV_D, MLA_Q_RANK, MLA_KV_RANK + MLA_ROPE])
        .tolist(), axis=-1)
    c_kv, k_rope = kv_lat[:, :MLA_KV_RANK], kv_lat[:, MLA_KV_RANK:]
    misc = _pad_last(jnp.concatenate([k_rope, dt], axis=-1), LANE)
    krot = _pad_last(_swap_halves(k_rope), LANE)
    out = jnp.concatenate([xbc, glu, gates, z, _pad_last(q_lat, MLA_Q_RANK_PAD), c_kv, misc, krot], axis=-1)
    return _pad_last(out, PROJ_W).astype(BF16)


def _mla_q_weight(w):
    w = w.reshape(MLA_Q_RANK, MLA_HEADS, MLA_QK)
    nope = w[:, :, :MLA_NOPE].reshape(MLA_Q_RANK, -1)
    rope = w[:, :, MLA_NOPE:]
    rope_p = _pad_last(rope, LANE).reshape(MLA_Q_RANK, -1)
    rot_p = _pad_last(_swap_halves(rope), LANE).reshape(MLA_Q_RANK, -1)
    out = jnp.concatenate([nope, rope_p, rot_p], axis=-1)
    return jnp.pad(out, [(0, MLA_Q_RANK_PAD - MLA_Q_RANK), (0, 0)]).astype(BF16)


def _mla_kv_weight(w):
    w = w.reshape(MLA_KV_RANK, MLA_HEADS, MLA_NOPE + MLA_V)
    return jnp.concatenate([w[:, :, :MLA_NOPE].reshape(MLA_KV_RANK, -1),
                            w[:, :, MLA_NOPE:].reshape(MLA_KV_RANK, -1)], axis=-1).astype(BF16)


def _row(a, width=None):
    a = a.reshape(1, -1).astype(F32)
    return a if width is None else _pad_last(a, width)


def _pick(n, prefs):
    for p in prefs:
        if n % p == 0:
            return p
    return n


def kernel(x, mem, positions, mix_norm_g, w_in, ssd_conv_w, ssd_conv_b, ssd_dt_bias, ssd_a_log, ssd_d, ssd_norm_g, ssd_w_out, conv_dw_w, conv_dw_b, conv_ln_g, conv_ln_b, conv_w_out, mla_q_a_g, mla_w_q_b, mla_kv_a_g, mla_w_kv_b, mla_q_norm_g, mla_k_norm_g, mla_w_o, gate_b, w_out, xattn_norm_g, mem_norm_g, xattn_w_q, xattn_w_kv, xattn_q_norm_g, xattn_k_norm_g, xattn_w_o, ffn_norm_g, ffn_w_in, ffn_w_out):
    b, s, d = x.shape
    t = b * s
    depth = w_in.shape[0]

    inv = ROPE_THETA ** (-jnp.arange(0, MLA_ROPE, 2, dtype=F32) / MLA_ROPE)
    ang = positions.astype(F32)[..., None] * inv
    cos, sin = jnp.cos(ang), jnp.sin(ang)
    cos2 = _pad_last(jnp.concatenate([cos, cos], axis=-1), LANE)
    sin2 = _pad_last(jnp.concatenate([-sin, sin], axis=-1), LANE)

    ssd_l = _pick(s, (128, 64))
    ssd_nc = 2 if s % (2 * ssd_l) == 0 else 1
    head_of_lane = np.arange(SSD_D_INNER) // SSD_HEAD_DIM
    expand = jnp.asarray((np.arange(LANE)[:, None] - MISC_DT) == head_of_lane[None, :], dtype=BF16)
    tri = jnp.asarray(np.tril(np.ones((ssd_l, ssd_l), np.float32)), dtype=BF16)

    tm_proj = _pick(t, (1024, 512, 256))
    tm_tok = _pick(t, (512, 256))
    tc = _pick(s, (256, 128))
    tq = _pick(s, (512, 256, 128))

    x2 = x.reshape(t, d)
    for l in range(depth):
        proj = _in_proj(x2, _row(mix_norm_g[l]), _in_proj_weight(w_in[l]), tm_proj, 1536)
        proj3 = proj.reshape(b, s, PROJ_W)

        dtb = jnp.pad(_row(ssd_dt_bias[l]), [(0, 0), (MISC_DT, LANE - MISC_DT - SSD_N_HEADS)])
        alog_s = jnp.pad(_row(ssd_a_log[l]), [(0, 0), (MISC_DT, LANE - MISC_DT - SSD_N_HEADS)])
        alog_e = jnp.repeat(_row(ssd_a_log[l]), SSD_HEAD_DIM, axis=-1)
        dskip_e = jnp.repeat(_row(ssd_d[l]), SSD_HEAD_DIM, axis=-1)
        ssd_pre = _ssd(proj3, ssd_conv_w[l], _row(ssd_conv_b[l]), dtb, alog_s, alog_e, dskip_e,
                       _row(ssd_norm_g[l]), expand, tri, ssd_l, ssd_nc)

        conv_pre = _conformer_conv(proj3, conv_dw_w[l], _row(conv_dw_b[l]), _row(conv_ln_g[l]),
                                   _row(conv_ln_b[l]), tc, 16)

        qg, kg = mla_q_norm_g[l], mla_k_norm_g[l]
        q, k, v = _mla_prep(
            proj3, cos2, sin2, _row(mla_q_a_g[l], MLA_Q_RANK_PAD), _mla_q_weight(mla_w_q_b[l]),
            _row(mla_kv_a_g[l]), _mla_kv_weight(mla_w_kv_b[l]),
            _row(qg[:MLA_NOPE]), _row(qg[MLA_NOPE:], LANE), _row(_swap_halves(qg[MLA_NOPE:]), LANE),
            _row(kg[:MLA_NOPE]), _row(kg[MLA_NOPE:], LANE), _row(_swap_halves(kg[MLA_NOPE:]), LANE), tm_tok)
        att = _attention(q, k, v, tq)

        x2 = _merge(x2, ssd_pre.reshape(t, d), conv_pre.reshape(t, d), att.reshape(t, d), proj,
                    gate_b[l].astype(F32), ssd_w_out[l].astype(BF16), conv_w_out[l].astype(BF16),
                    mla_w_o[l].astype(BF16), w_out[l].astype(BF16), tm_tok)

        mk, mv = _mem_kv(mem, _row(mem_norm_g[l]), xattn_w_kv[l].astype(BF16), _row(xattn_k_norm_g[l]))
        x2 = _xattn(x2.reshape(b, s, d), mk, mv, _row(xattn_norm_g[l]), xattn_w_q[l].astype(BF16),
                    _row(xattn_q_norm_g[l]), xattn_w_o[l].astype(BF16), tm_tok).reshape(t, d)

        w_ffn = ffn_w_in[l].astype(BF16)
        x2 = _ffn(x2, _row(ffn_norm_g[l]), w_ffn[:, :FFN_HIDDEN], w_ffn[:, FFN_HIDDEN:],
                  ffn_w_out[l].astype(BF16), tm_tok, 256)
    return x2.reshape(b, s, d)
```

```python
import functools
import math

import jax
import jax.numpy as jnp
import numpy as np
from jax import lax
from jax.experimental import pallas as pl
from jax.experimental.pallas import tpu as pltpu

F32 = jnp.float32
BF16 = jnp.bfloat16

EPS = 1e-6
CHUNK = 64
D_MODEL = 1024

SSD_D_INNER = 1024
SSD_HEAD_DIM = 64
SSD_N_HEADS = 16
SSD_N_GROUPS = 4
SSD_HEADS_PER_GROUP = 4
SSD_D_STATE = 128
SSD_CONV = 4
SSD_XBC = 2048
SSD_GROUP_W = SSD_HEADS_PER_GROUP * SSD_HEAD_DIM

CONV_D = 1024
CONV_K = 31
CONV_HALO = 32

MLA_HEADS = 8
MLA_Q_RANK = 384
MLA_Q_RANK_PAD = 512
MLA_KV_RANK = 256
MLA_NOPE = 128
MLA_ROPE = 64
MLA_V = 128
MLA_QK = MLA_NOPE + MLA_ROPE
MLA_QK_PAD = 256
ROPE_THETA = 10000.0

X_HEADS = 4
X_HEAD_DIM = 256
FFN_HIDDEN = 2816

LANE = 128
SUBLANE = 8

OFF_XBC = 0
OFF_GLU_A = 2048
OFF_GLU_G = 3072
OFF_GATE = 4096
OFF_Z = 7168
OFF_QLAT = 8192
OFF_CKV = 8704
OFF_MISC = 8960
OFF_KROT = 9088
PROJ_W = 9216
MISC_DT = 64

VMEM_LIMIT = 56 * 1024 * 1024


def _cparams(sem):
    return pltpu.CompilerParams(dimension_semantics=sem, vmem_limit_bytes=VMEM_LIMIT)


def _dot(a, b):
    return jnp.dot(a, b, preferred_element_type=F32)


def _dot_nt(a, b):
    return lax.dot_general(a, b, (((1,), (1,)), ((), ())), preferred_element_type=F32)


def _split3(x):
    hi = x.astype(BF16)
    r = x - hi.astype(F32)
    mid = r.astype(BF16)
    lo = (r - mid.astype(F32)).astype(BF16)
    return hi, mid, lo


def _dot_exact_rhs01(x, m01):
    hi, mid, lo = _split3(x)
    return _dot(hi, m01) + _dot(mid, m01) + _dot(lo, m01)


def _dot_exact_lhs01(m01, x):
    hi, mid, lo = _split3(x)
    return _dot(m01, hi) + _dot(m01, mid) + _dot(m01, lo)


def _sigmoid(x):
    return 1.0 / (1.0 + jnp.exp(-x))


def _silu(x):
    return x * _sigmoid(x)


def _in_proj_kernel(x_ref, g_ref, w_ref, o_ref, u_ref):
    @pl.when(pl.program_id(1) == 0)
    def _():
        x = x_ref[...]
        ms = jnp.mean(x * x, axis=-1, keepdims=True)
        u_ref[...] = (x * lax.rsqrt(ms + EPS) * g_ref[...]).astype(BF16)

    o_ref[...] = _dot(u_ref[...], w_ref[...]).astype(o_ref.dtype)


def _in_proj(x2, g, w, tm, tn):
    t, d = x2.shape
    n = w.shape[1]
    return pl.pallas_call(
        _in_proj_kernel,
        grid=(t // tm, n // tn),
        in_specs=[
            pl.BlockSpec((tm, d), lambda i, j: (i, 0)),
            pl.BlockSpec((1, d), lambda i, j: (0, 0)),
            pl.BlockSpec((d, tn), lambda i, j: (0, j)),
        ],
        out_specs=pl.BlockSpec((tm, tn), lambda i, j: (i, j)),
        out_shape=jax.ShapeDtypeStruct((t, n), F32),
        scratch_shapes=[pltpu.VMEM((tm, d), BF16)],
        compiler_params=_cparams(("parallel", "arbitrary")),
        name="in_proj",
    )(x2, g, w)


def _ssd_kernel(xbc_ref, z_ref, misc_ref, convw_ref, convb_ref, dtb_ref, alog_s_ref, alog_e_ref,
                dskip_ref, normg_ref, expand_ref, tri_ref, y_ref, xpad_ref, state_ref, *, L, n_chunks):
    ts = L * n_chunks
    pad = SUBLANE

    @pl.when(pl.program_id(1) == 0)
    def _():
        xpad_ref[0:pad, :] = jnp.zeros((pad, SSD_XBC), F32)
        state_ref[...] = jnp.zeros(state_ref.shape, F32)

    xpad_ref[pad:pad + ts, :] = xbc_ref[0]
    conv = jnp.broadcast_to(convb_ref[...], (ts, SSD_XBC))
    for k in range(SSD_CONV):
        conv = conv + convw_ref[k:k + 1, :] * xpad_ref[pl.ds(pad - (SSD_CONV - 1) + k, ts), :]
    xpad_ref[0:pad, :] = xpad_ref[ts:ts + pad, :]
    xc = _silu(conv)

    misc = misc_ref[0]
    v = misc + dtb_ref[...]
    dt_s = jnp.maximum(v, 0.0) + jnp.log1p(jnp.exp(-jnp.abs(v)))
    a_neg_s = -jnp.exp(alog_s_ref[...])
    a_neg_e = -jnp.exp(alog_e_ref[...])
    expand = expand_ref[...]
    tri = tri_ref[...]
    dt_e = _dot_exact_rhs01(dt_s, expand)
    a_s = dt_s * a_neg_s
    a_e = dt_e * a_neg_e

    row_i = lax.broadcasted_iota(jnp.int32, (L, L), 0)
    col_i = lax.broadcasted_iota(jnp.int32, (L, L), 1)
    causal = row_i >= col_i
    lane_g = lax.broadcasted_iota(jnp.int32, (1, SSD_GROUP_W), 1) // SSD_HEAD_DIM

    z = z_ref[0]
    for c in range(n_chunks):
        r0 = c * L
        xs = xc[r0:r0 + L, 0:SSD_D_INNER]
        acs_s = _dot_exact_lhs01(tri, a_s[r0:r0 + L])
        acs_e = _dot_exact_lhs01(tri, a_e[r0:r0 + L])
        acs_st = acs_s.T
        last_e = acs_e[L - 1:L, :]
        exp_acs = jnp.exp(acs_e)
        to_end = jnp.exp(last_e - acs_e)
        chunk_decay = jnp.exp(last_e)
        xdt = xs * dt_e[r0:r0 + L]
        xdt_b = xdt.astype(BF16)
        xend_b = (xdt * to_end).astype(BF16)
        y_groups = []
        for g in range(SSD_N_GROUPS):
            gs = slice(g * SSD_GROUP_W, (g + 1) * SSD_GROUP_W)
            b_g = xc[r0:r0 + L, SSD_D_INNER + g * SSD_D_STATE:SSD_D_INNER + (g + 1) * SSD_D_STATE]
            c_g = xc[r0:r0 + L, SSD_D_INNER + (SSD_N_GROUPS + g) * SSD_D_STATE:
                     SSD_D_INNER + (SSD_N_GROUPS + g + 1) * SSD_D_STATE]
            b_gt = b_g.T.astype(BF16)
            c_gb = c_g.astype(BF16)
            cb = _dot(c_gb, b_gt)
            st = state_ref[g]
            y_g = _dot(c_gb, st.astype(BF16)) * exp_acs[:, gs]
            x_g = xdt_b[:, gs]
            for r in range(SSD_HEADS_PER_GROUP):
                h = g * SSD_HEADS_PER_GROUP + r
                col = acs_s[:, MISC_DT + h:MISC_DT + h + 1]
                row = acs_st[MISC_DT + h:MISC_DT + h + 1, :]
                decay = jnp.exp(jnp.where(causal, col - row, -jnp.inf))
                gm = (cb * decay).astype(BF16)
                x_h = jnp.where(lane_g == r, x_g, jnp.zeros_like(x_g))
                y_g = y_g + _dot(gm, x_h)
            state_ref[g] = st * chunk_decay[:, gs] + _dot(b_gt, xend_b[:, gs])
            y_groups.append(y_g)
        y = jnp.concatenate(y_groups, axis=-1) + dskip_ref[...] * xs
        y = y * _silu(z[r0:r0 + L])
        outs = []
        for g in range(SSD_N_GROUPS):
            yg = y[:, g * SSD_GROUP_W:(g + 1) * SSD_GROUP_W]
            outs.append(yg * lax.rsqrt(jnp.mean(yg * yg, axis=-1, keepdims=True) + EPS))
        y = jnp.concatenate(outs, axis=-1) * normg_ref[...]
        y_ref[0, r0:r0 + L, :] = y.astype(y_ref.dtype)


def _ssd(proj3, convw, convb, dtb, alog_s, alog_e, dskip_e, normg, expand, tri, L, n_chunks):
    b, s, _ = proj3.shape
    ts = L * n_chunks
    full = lambda shape: pl.BlockSpec(shape, lambda bi, si: (0,) * len(shape))
    return pl.pallas_call(
        functools.partial(_ssd_kernel, L=L, n_chunks=n_chunks),
        grid=(b, s // ts),
        in_specs=[
            pl.BlockSpec((1, ts, SSD_XBC), lambda bi, si: (bi, si, OFF_XBC // SSD_XBC)),
            pl.BlockSpec((1, ts, SSD_D_INNER), lambda bi, si: (bi, si, OFF_Z // SSD_D_INNER)),
            pl.BlockSpec((1, ts, LANE), lambda bi, si: (bi, si, OFF_MISC // LANE)),
            full((SSD_CONV, SSD_XBC)), full((1, SSD_XBC)), full((1, LANE)), full((1, LANE)),
            full((1, SSD_D_INNER)), full((1, SSD_D_INNER)), full((1, SSD_D_INNER)),
            full((LANE, SSD_D_INNER)), full((L, L)),
        ],
        out_specs=pl.BlockSpec((1, ts, SSD_D_INNER), lambda bi, si: (bi, si, 0)),
        out_shape=jax.ShapeDtypeStruct((b, s, SSD_D_INNER), BF16),
        scratch_shapes=[
            pltpu.VMEM((SUBLANE + ts, SSD_XBC), F32),
            pltpu.VMEM((SSD_N_GROUPS, SSD_D_STATE, SSD_GROUP_W), F32),
        ],
        compiler_params=_cparams(("parallel", "arbitrary")),
        name="ssd",
    )(proj3, proj3, proj3, convw, convb, dtb, alog_s, alog_e, dskip_e, normg, expand, tri)


def _conv_kernel(a_ref, g_ref, w_ref, b_ref, lng_ref, lnb_ref, o_ref, vpad_ref, sh_ref, acc_ref, *, tc, rb):
    n = CONV_HALO + tc
    n_cb = CONV_D // LANE

    @pl.when(pl.program_id(1) == 0)
    def _():
        vpad_ref[:, 0:CONV_HALO, :] = jnp.zeros((n_cb, CONV_HALO, LANE), F32)

    v = a_ref[0] * _sigmoid(g_ref[0])
    first = CONV_HALO - (CONV_K - 1)
    for cb in range(n_cb):
        ls = slice(cb * LANE, (cb + 1) * LANE)
        vpad_ref[cb, CONV_HALO:n, :] = v[:, ls]
        vp = vpad_ref[cb, 0:n, :]
        for r in range(1, SUBLANE):
            sh_ref[r - 1, cb, 0:n, :] = pltpu.roll(vp, n - r, axis=0)

        def row_block(i, carry, cb=cb, ls=ls):
            r0 = pl.multiple_of(i * rb, rb)
            acc = jnp.broadcast_to(b_ref[:, ls], (rb, LANE))
            for j in range(CONV_K):
                q, r = divmod(first + j, SUBLANE)
                start = pl.multiple_of(r0 + q * SUBLANE, SUBLANE)
                win = vpad_ref[cb, pl.ds(start, rb), :] if r == 0 else sh_ref[r - 1, cb, pl.ds(start, rb), :]
                acc = acc + w_ref[j:j + 1, ls] * win
            acc_ref[cb, pl.ds(r0, rb), :] = acc
            return carry

        lax.fori_loop(0, tc // rb, row_block, 0)
        vpad_ref[cb, 0:CONV_HALO, :] = vpad_ref[cb, tc:n, :]
    v = jnp.concatenate([acc_ref[cb] for cb in range(n_cb)], axis=-1)
    mu = jnp.mean(v, axis=-1, keepdims=True)
    vc = v - mu
    var = jnp.mean(vc * vc, axis=-1, keepdims=True)
    y = vc * lax.rsqrt(var + EPS) * lng_ref[...] + lnb_ref[...]
    o_ref[0] = _silu(y).astype(o_ref.dtype)


def _conformer_conv(proj3, w, bias, lng, lnb, tc, rb):
    b, s, _ = proj3.shape
    full = lambda shape: pl.BlockSpec(shape, lambda bi, si: (0,) * len(shape))
    return pl.pallas_call(
        functools.partial(_conv_kernel, tc=tc, rb=rb),
        grid=(b, s // tc),
        in_specs=[
            pl.BlockSpec((1, tc, CONV_D), lambda bi, si: (bi, si, OFF_GLU_A // CONV_D)),
            pl.BlockSpec((1, tc, CONV_D), lambda bi, si: (bi, si, OFF_GLU_G // CONV_D)),
            full((CONV_K, CONV_D)), full((1, CONV_D)), full((1, CONV_D)), full((1, CONV_D)),
        ],
        out_specs=pl.BlockSpec((1, tc, CONV_D), lambda bi, si: (bi, si, 0)),
        out_shape=jax.ShapeDtypeStruct((b, s, CONV_D), BF16),
        scratch_shapes=[pltpu.VMEM((CONV_D // LANE, CONV_HALO + tc + SUBLANE, LANE), F32),
                        pltpu.VMEM((SUBLANE - 1, CONV_D // LANE, CONV_HALO + tc + SUBLANE, LANE), F32),
                        pltpu.VMEM((CONV_D // LANE, tc, LANE), F32)],
        compiler_params=_cparams(("parallel", "arbitrary")),
        name="conformer_conv",
    )(proj3, proj3, w, bias, lng, lnb)


def _mla_prep_kernel(qlat_ref, ckv_ref, misc_ref, krot_ref, cos_ref, sin_ref, qag_ref, wq_ref, kvag_ref,
                     wkv_ref, qg_nope_ref, qg_rope_ref, qg_rot_ref, kg_nope_ref, kg_rope_ref, kg_rot_ref,
                     q_ref, k_ref, v_ref):
    hn = MLA_HEADS * MLA_NOPE
    cos2 = cos_ref[0]
    sin2 = sin_ref[0]
    scale = MLA_QK ** -0.5 * math.log2(math.e)

    ql = qlat_ref[0]
    ms = jnp.sum(ql * ql, axis=-1, keepdims=True) * (1.0 / MLA_Q_RANK)
    qa = (ql * lax.rsqrt(ms + EPS) * qag_ref[...]).astype(BF16)
    qf = _dot(qa, wq_ref[...])
    for h in range(MLA_HEADS):
        qn = qf[:, h * LANE:(h + 1) * LANE]
        qn = qn * lax.rsqrt(jnp.mean(qn * qn, axis=-1, keepdims=True) + EPS) * qg_nope_ref[...]
        qr = qf[:, hn + h * LANE:hn + (h + 1) * LANE]
        qs = qf[:, 2 * hn + h * LANE:2 * hn + (h + 1) * LANE]
        inv = lax.rsqrt(jnp.sum(qr * qr, axis=-1, keepdims=True) * (1.0 / MLA_ROPE) + EPS)
        qro = (qr * qg_rope_ref[...] * cos2 + qs * qg_rot_ref[...] * sin2) * inv
        q_ref[0, :, h * MLA_QK_PAD:h * MLA_QK_PAD + LANE] = (qn * scale).astype(q_ref.dtype)
        q_ref[0, :, h * MLA_QK_PAD + LANE:(h + 1) * MLA_QK_PAD] = (qro * scale).astype(q_ref.dtype)

    ckv = ckv_ref[0]
    ca = (ckv * lax.rsqrt(jnp.mean(ckv * ckv, axis=-1, keepdims=True) + EPS) * kvag_ref[...]).astype(BF16)
    kvf = _dot(ca, wkv_ref[...])
    lane = lax.broadcasted_iota(jnp.int32, (1, LANE), 1)
    kr = jnp.where(lane < MLA_ROPE, misc_ref[0], 0.0)
    ks = krot_ref[0]
    inv = lax.rsqrt(jnp.sum(kr * kr, axis=-1, keepdims=True) * (1.0 / MLA_ROPE) + EPS)
    kro = ((kr * kg_rope_ref[...] * cos2 + ks * kg_rot_ref[...] * sin2) * inv).astype(k_ref.dtype)
    for h in range(MLA_HEADS):
        kn = kvf[:, h * LANE:(h + 1) * LANE]
        kn = kn * lax.rsqrt(jnp.mean(kn * kn, axis=-1, keepdims=True) + EPS) * kg_nope_ref[...]
        k_ref[0, :, h * MLA_QK_PAD:h * MLA_QK_PAD + LANE] = kn.astype(k_ref.dtype)
        k_ref[0, :, h * MLA_QK_PAD + LANE:(h + 1) * MLA_QK_PAD] = kro
    v_ref[0] = kvf[:, hn:].astype(v_ref.dtype)


def _mla_prep(proj3, cos2, sin2, qag, wq, kvag, wkv, qg_nope, qg_rope, qg_rot, kg_nope, kg_rope, kg_rot, tm):
    b, s, _ = proj3.shape
    full = lambda shape: pl.BlockSpec(shape, lambda bi, si: (0,) * len(shape))
    hq = MLA_HEADS * MLA_QK_PAD
    hv = MLA_HEADS * MLA_V
    return pl.pallas_call(
        _mla_prep_kernel,
        grid=(b, s // tm),
        in_specs=[
            pl.BlockSpec((1, tm, MLA_Q_RANK_PAD), lambda bi, si: (bi, si, OFF_QLAT // MLA_Q_RANK_PAD)),
            pl.BlockSpec((1, tm, MLA_KV_RANK), lambda bi, si: (bi, si, OFF_CKV // MLA_KV_RANK)),
            pl.BlockSpec((1, tm, LANE), lambda bi, si: (bi, si, OFF_MISC // LANE)),
            pl.BlockSpec((1, tm, LANE), lambda bi, si: (bi, si, OFF_KROT // LANE)),
            pl.BlockSpec((1, tm, LANE), lambda bi, si: (bi, si, 0)),
            pl.BlockSpec((1, tm, LANE), lambda bi, si: (bi, si, 0)),
            full((1, MLA_Q_RANK_PAD)), full(wq.shape), full((1, MLA_KV_RANK)), full(wkv.shape),
            full((1, LANE)), full((1, LANE)), full((1, LANE)), full((1, LANE)), full((1, LANE)), full((1, LANE)),
        ],
        out_specs=[
            pl.BlockSpec((1, tm, hq), lambda bi, si: (bi, si, 0)),
            pl.BlockSpec((1, tm, hq), lambda bi, si: (bi, si, 0)),
            pl.BlockSpec((1, tm, hv), lambda bi, si: (bi, si, 0)),
        ],
        out_shape=[
            jax.ShapeDtypeStruct((b, s, hq), BF16),
            jax.ShapeDtypeStruct((b, s, hq), BF16),
            jax.ShapeDtypeStruct((b, s, hv), BF16),
        ],
        compiler_params=_cparams(("parallel", "parallel")),
        name="mla_prep",
    )(proj3, proj3, proj3, proj3, cos2, sin2, qag, wq, kvag, wkv, qg_nope, qg_rope, qg_rot, kg_nope, kg_rope,
      kg_rot)


def _attn_kernel(q_ref, k_ref, v_ref, o_ref, sa_ref, sb_ref, m_ref, acc_ref, *, tq):
    i = pl.program_id(2)
    q = q_ref[0]

    def scores(j, s_ref):
        start = pl.multiple_of(j * tq, tq)
        s_ref[...] = _dot_nt(q, k_ref[0, pl.ds(start, tq), :])

    def consume(j, s_ref, masked):
        start = pl.multiple_of(j * tq, tq)
        v = v_ref[0, pl.ds(start, tq), :]
        v1 = jnp.concatenate([v, jnp.ones_like(v)], axis=-1)
        s = s_ref[...]
        if masked:
            rq = lax.broadcasted_iota(jnp.int32, (tq, tq), 0) // CHUNK
            ck = lax.broadcasted_iota(jnp.int32, (tq, tq), 1) // CHUNK
            s = jnp.where(ck <= rq, s, -jnp.inf)
        m = m_ref[...]
        m_new = jnp.maximum(m, jnp.max(s, axis=-1, keepdims=True))
        p = jnp.exp2(s - m_new)
        acc_ref[...] = jnp.exp2(m - m_new) * acc_ref[...] + _dot(p.astype(BF16), v1)
        m_ref[...] = m_new

    m_ref[...] = jnp.full(m_ref.shape, -jnp.inf, F32)
    acc_ref[...] = jnp.zeros(acc_ref.shape, F32)
    scores(0, sa_ref)

    def pair(jj, carry):
        j = 2 * jj
        scores(j + 1, sb_ref)
        consume(j, sa_ref, False)
        scores(j + 2, sa_ref)
        consume(j + 1, sb_ref, False)
        return carry

    lax.fori_loop(0, i // 2, pair, 0)

    @pl.when(i % 2 == 0)
    def _():
        consume(i, sa_ref, True)

    @pl.when(i % 2 == 1)
    def _():
        scores(i, sb_ref)
        consume(i - 1, sa_ref, False)
        consume(i, sb_ref, True)

    acc = acc_ref[...]
    o_ref[0] = (acc[:, :MLA_V] / acc[:, MLA_V:]).astype(o_ref.dtype)


def _attention(q, k, v, tq):
    b, s, _ = q.shape
    return pl.pallas_call(
        functools.partial(_attn_kernel, tq=tq),
        grid=(b, MLA_HEADS, s // tq),
        in_specs=[
            pl.BlockSpec((1, tq, MLA_QK_PAD), lambda bi, h, i: (bi, i, h)),
            pl.BlockSpec((1, s, MLA_QK_PAD), lambda bi, h, i: (bi, 0, h)),
            pl.BlockSpec((1, s, MLA_V), lambda bi, h, i: (bi, 0, h)),
        ],
        out_specs=pl.BlockSpec((1, tq, MLA_V), lambda bi, h, i: (bi, i, h)),
        out_shape=jax.ShapeDtypeStruct((b, s, MLA_HEADS * MLA_V), BF16),
        scratch_shapes=[pltpu.VMEM((tq, tq), F32), pltpu.VMEM((tq, tq), F32), pltpu.VMEM((tq, 1), F32),
                        pltpu.VMEM((tq, 2 * MLA_V), F32)],
        compiler_params=_cparams(("parallel", "parallel", "arbitrary")),
        name="mla_attention",
    )(q, k, v)


def _merge_kernel(x_ref, ssd_ref, conv_ref, att_ref, g0_ref, g1_ref, g2_ref, gb_ref, w_ssd_ref, w_conv_ref,
                  w_mla_ref, w_out_ref, o_ref):
    merged = _sigmoid(g0_ref[...] + gb_ref[0:1, :]) * _dot(ssd_ref[...], w_ssd_ref[...])
    merged = merged + _sigmoid(g1_ref[...] + gb_ref[1:2, :]) * _dot(conv_ref[...], w_conv_ref[...])
    merged = merged + _sigmoid(g2_ref[...] + gb_ref[2:3, :]) * _dot(att_ref[...], w_mla_ref[...])
    o_ref[...] = x_ref[...] + _dot(merged.astype(BF16), w_out_ref[...])


def _merge(x2, ssd_pre, conv_pre, att, proj2, gate_b, w_ssd, w_conv, w_mla, w_out, tm):
    t, d = x2.shape
    row = lambda c: pl.BlockSpec((tm, d), lambda i, c=c: (i, c))
    full = lambda shape: pl.BlockSpec(shape, lambda i: (0,) * len(shape))
    g0 = OFF_GATE // d
    return pl.pallas_call(
        _merge_kernel,
        grid=(t // tm,),
        in_specs=[row(0), row(0), row(0), row(0), row(g0), row(g0 + 1), row(g0 + 2), full(gate_b.shape),
                  full((d, d)), full((d, d)), full((d, d)), full((d, d))],
        out_specs=row(0),
        out_shape=jax.ShapeDtypeStruct((t, d), F32),
        compiler_params=_cparams(("parallel",)),
        name="merge",
    )(x2, ssd_pre, conv_pre, att, proj2, proj2, proj2, gate_b, w_ssd, w_conv, w_mla, w_out)


def _mem_kv_kernel(mem_ref, g_ref, w_ref, kg_ref, k_ref, v_ref):
    m = mem_ref[0]
    mn = (m * lax.rsqrt(jnp.mean(m * m, axis=-1, keepdims=True) + EPS) * g_ref[...]).astype(BF16)
    kv = _dot(mn, w_ref[...])
    for h in range(X_HEADS):
        kh = kv[:, h * X_HEAD_DIM:(h + 1) * X_HEAD_DIM]
        kh = kh * lax.rsqrt(jnp.mean(kh * kh, axis=-1, keepdims=True) + EPS) * kg_ref[...]
        k_ref[0, :, h * X_HEAD_DIM:(h + 1) * X_HEAD_DIM] = kh.astype(k_ref.dtype)
    v_ref[0] = kv[:, D_MODEL:].astype(v_ref.dtype)


def _mem_kv(mem, g, w_kv, kg):
    b, m, d = mem.shape
    full = lambda shape: pl.BlockSpec(shape, lambda bi: (0,) * len(shape))
    blk = pl.BlockSpec((1, m, d), lambda bi: (bi, 0, 0))
    return pl.pallas_call(
        _mem_kv_kernel,
        grid=(b,),
        in_specs=[blk, full((1, d)), full(w_kv.shape), full((1, X_HEAD_DIM))],
        out_specs=[blk, blk],
        out_shape=[jax.ShapeDtypeStruct((b, m, d), BF16), jax.ShapeDtypeStruct((b, m, d), BF16)],
        compiler_params=_cparams(("parallel",)),
        name="mem_kv",
    )(mem, g, w_kv, kg)


def _xattn_kernel(x_ref, k_ref, v_ref, g_ref, wq_ref, qg_ref, wo_ref, o_ref):
    x = x_ref[0]
    u = (x * lax.rsqrt(jnp.mean(x * x, axis=-1, keepdims=True) + EPS) * g_ref[...]).astype(BF16)
    q = _dot(u, wq_ref[...])
    scale = X_HEAD_DIM ** -0.5
    outs = []
    for h in range(X_HEADS):
        hs = slice(h * X_HEAD_DIM, (h + 1) * X_HEAD_DIM)
        qh = q[:, hs]
        qh = qh * lax.rsqrt(jnp.mean(qh * qh, axis=-1, keepdims=True) + EPS) * (qg_ref[...] * scale)
        s = _dot_nt(qh.astype(BF16), k_ref[0, :, hs])
        p = jnp.exp(s - jnp.max(s, axis=-1, keepdims=True))
        o = _dot(p.astype(BF16), v_ref[0, :, hs])
        outs.append((o / jnp.sum(p, axis=-1, keepdims=True)).astype(BF16))
    o_ref[0] = x + _dot(jnp.concatenate(outs, axis=-1), wo_ref[...])


def _xattn(x3, k, v, g, wq, qg, wo, tm):
    b, s, d = x3.shape
    m = k.shape[1]
    full = lambda shape: pl.BlockSpec(shape, lambda bi, si: (0,) * len(shape))
    return pl.pallas_call(
        _xattn_kernel,
        grid=(b, s // tm),
        in_specs=[
            pl.BlockSpec((1, tm, d), lambda bi, si: (bi, si, 0)),
            pl.BlockSpec((1, m, d), lambda bi, si: (bi, 0, 0)),
            pl.BlockSpec((1, m, d), lambda bi, si: (bi, 0, 0)),
            full((1, d)), full((d, d)), full((1, X_HEAD_DIM)), full((d, d)),
        ],
        out_specs=pl.BlockSpec((1, tm, d), lambda bi, si: (bi, si, 0)),
        out_shape=jax.ShapeDtypeStruct((b, s, d), F32),
        compiler_params=_cparams(("parallel", "parallel")),
        name="mem_xattn",
    )(x3, k, v, g, wq, qg, wo)


def _ffn_kernel(x_ref, g_ref, wg_ref, wu_ref, wo_ref, o_ref, *, hc):
    x = x_ref[...]
    u = (x * lax.rsqrt(jnp.mean(x * x, axis=-1, keepdims=True) + EPS) * g_ref[...]).astype(BF16)
    acc = x
    for c in range(0, FFN_HIDDEN, hc):
        gate = _dot(u, wg_ref[:, c:c + hc])
        up = _dot(u, wu_ref[:, c:c + hc])
        acc = acc + _dot((_silu(gate) * up).astype(BF16), wo_ref[c:c + hc, :])
    o_ref[...] = acc


def _ffn(x2, g, wg, wu, wo, tm, hc):
    t, d = x2.shape
    full = lambda shape: pl.BlockSpec(shape, lambda i: (0,) * len(shape))
    return pl.pallas_call(
        functools.partial(_ffn_kernel, hc=hc),
        grid=(t // tm,),
        in_specs=[pl.BlockSpec((tm, d), lambda i: (i, 0)), full((1, d)), full(wg.shape), full(wu.shape),
                  full(wo.shape)],
        out_specs=pl.BlockSpec((tm, d), lambda i: (i, 0)),
        out_shape=jax.ShapeDtypeStruct((t, d), F32),
        compiler_params=_cparams(("parallel",)),
        name="ffn",
    )(x2, g, wg, wu, wo)


def _swap_halves(a):
    h = a.shape[-1] // 2
    return jnp.concatenate([a[..., h:], a[..., :h]], axis=-1)


def _pad_last(a, width):
    return jnp.pad(a, [(0, 0)] * (a.ndim - 1) + [(0, width - a.shape[-1])])


def _in_proj_weight(w):
    z, xbc, dt, glu, q_lat, kv_lat, gates = jnp.split(
        w, np.cumsum([SSD_D_INNER, SSD_XBC, SSD_N_HEADS, 2 * CONV_D, MLA_Q_RANK, MLA_KV_RANK + MLA_ROPE])
        .tolist(), axis=-1)
    c_kv, k_rope = kv_lat[:, :MLA_KV_RANK], kv_lat[:, MLA_KV_RANK:]
    misc = _pad_last(jnp.concatenate([k_rope, dt], axis=-1), LANE)
    krot = _pad_last(_swap_halves(k_rope), LANE)
    out = jnp.concatenate([xbc, glu, gates, z, _pad_last(q_lat, MLA_Q_RANK_PAD), c_kv, misc, krot], axis=-1)
    return _pad_last(out, PROJ_W).astype(BF16)


def _mla_q_weight(w):
    w = w.reshape(MLA_Q_RANK, MLA_HEADS, MLA_QK)
    nope = w[:, :, :MLA_NOPE].reshape(MLA_Q_RANK, -1)
    rope = w[:, :, MLA_NOPE:]
    rope_p = _pad_last(rope, LANE).reshape(MLA_Q_RANK, -1)
    rot_p = _pad_last(_swap_halves(rope), LANE).reshape(MLA_Q_RANK, -1)
    out = jnp.concatenate([nope, rope_p, rot_p], axis=-1)
    return jnp.pad(out, [(0, MLA_Q_RANK_PAD - MLA_Q_RANK), (0, 0)]).astype(BF16)


def _mla_kv_weight(w):
    w = w.reshape(MLA_KV_RANK, MLA_HEADS, MLA_NOPE + MLA_V)
    return jnp.concatenate([w[:, :, :MLA_NOPE].reshape(MLA_KV_RANK, -1),
                            w[:, :, MLA_NOPE:].reshape(MLA_KV_RANK, -1)], axis=-1).astype(BF16)


def _row(a, width=None):
    a = a.reshape(1, -1).astype(F32)
    return a if width is None else _pad_last(a, width)


def _pick(n, prefs):
    for p in prefs:
        if n % p == 0:
            return p
    return n


def kernel(x, mem, positions, mix_norm_g, w_in, ssd_conv_w, ssd_conv_b, ssd_dt_bias, ssd_a_log, ssd_d, ssd_norm_g, ssd_w_out, conv_dw_w, conv_dw_b, conv_ln_g, conv_ln_b, conv_w_out, mla_q_a_g, mla_w_q_b, mla_kv_a_g, mla_w_kv_b, mla_q_norm_g, mla_k_norm_g, mla_w_o, gate_b, w_out, xattn_norm_g, mem_norm_g, xattn_w_q, xattn_w_kv, xattn_q_norm_g, xattn_k_norm_g, xattn_w_o, ffn_norm_g, ffn_w_in, ffn_w_out):
    b, s, d = x.shape
    t = b * s
    depth = w_in.shape[0]

    inv = ROPE_THETA ** (-jnp.arange(0, MLA_ROPE, 2, dtype=F32) / MLA_ROPE)
    ang = positions.astype(F32)[..., None] * inv
    cos, sin = jnp.cos(ang), jnp.sin(ang)
    cos2 = _pad_last(jnp.concatenate([cos, cos], axis=-1), LANE)
    sin2 = _pad_last(jnp.concatenate([-sin, sin], axis=-1), LANE)

    ssd_l = _pick(s, (128, 64))
    ssd_nc = 2 if s % (2 * ssd_l) == 0 else 1
    head_of_lane = np.arange(SSD_D_INNER) // SSD_HEAD_DIM
    expand = jnp.asarray((np.arange(LANE)[:, None] - MISC_DT) == head_of_lane[None, :], dtype=BF16)
    tri = jnp.asarray(np.tril(np.ones((ssd_l, ssd_l), np.float32)), dtype=BF16)

    tm_proj = _pick(t, (1024, 512, 256))
    tm_tok = _pick(t, (512, 256))
    tc = _pick(s, (256, 128))
    tq = _pick(s, (512, 256, 128))

    x2 = x.reshape(t, d)
    for l in range(depth):
        proj = _in_proj(x2, _row(mix_norm_g[l]), _in_proj_weight(w_in[l]), tm_proj, 1536)
        proj3 = proj.reshape(b, s, PROJ_W)

        dtb = jnp.pad(_row(ssd_dt_bias[l]), [(0, 0), (MISC_DT, LANE - MISC_DT - SSD_N_HEADS)])
        alog_s = jnp.pad(_row(ssd_a_log[l]), [(0, 0), (MISC_DT, LANE - MISC_DT - SSD_N_HEADS)])
        alog_e = jnp.repeat(_row(ssd_a_log[l]), SSD_HEAD_DIM, axis=-1)
        dskip_e = jnp.repeat(_row(ssd_d[l]), SSD_HEAD_DIM, axis=-1)
        ssd_pre = _ssd(proj3, ssd_conv_w[l], _row(ssd_conv_b[l]), dtb, alog_s, alog_e, dskip_e,
                       _row(ssd_norm_g[l]), expand, tri, ssd_l, ssd_nc)

        conv_pre = _conformer_conv(proj3, conv_dw_w[l], _row(conv_dw_b[l]), _row(conv_ln_g[l]),
                                   _row(conv_ln_b[l]), tc, 64)

        qg, kg = mla_q_norm_g[l], mla_k_norm_g[l]
        q, k, v = _mla_prep(
            proj3, cos2, sin2, _row(mla_q_a_g[l], MLA_Q_RANK_PAD), _mla_q_weight(mla_w_q_b[l]),
            _row(mla_kv_a_g[l]), _mla_kv_weight(mla_w_kv_b[l]),
            _row(qg[:MLA_NOPE]), _row(qg[MLA_NOPE:], LANE), _row(_swap_halves(qg[MLA_NOPE:]), LANE),
            _row(kg[:MLA_NOPE]), _row(kg[MLA_NOPE:], LANE), _row(_swap_halves(kg[MLA_NOPE:]), LANE), tm_tok)
        att = _attention(q, k, v, tq)

        x2 = _merge(x2, ssd_pre.reshape(t, d), conv_pre.reshape(t, d), att.reshape(t, d), proj,
                    gate_b[l].astype(F32), ssd_w_out[l].astype(BF16), conv_w_out[l].astype(BF16),
                    mla_w_o[l].astype(BF16), w_out[l].astype(BF16), tm_tok)

        mk, mv = _mem_kv(mem, _row(mem_norm_g[l]), xattn_w_kv[l].astype(BF16), _row(xattn_k_norm_g[l]))
        x2 = _xattn(x2.reshape(b, s, d), mk, mv, _row(xattn_norm_g[l]), xattn_w_q[l].astype(BF16),
                    _row(xattn_q_norm_g[l]), xattn_w_o[l].astype(BF16), tm_tok).reshape(t, d)

        w_ffn = ffn_w_in[l].astype(BF16)
        x2 = _ffn(x2, _row(ffn_norm_g[l]), w_ffn[:, :FFN_HIDDEN], w_ffn[:, FFN_HIDDEN:],
                  ffn_w_out[l].astype(BF16), tm_tok, 256)
    return x2.reshape(b, s, d)
```

```python
import functools
import math

import jax
import jax.numpy as jnp
import numpy as np
from jax import lax
from jax.experimental import pallas as pl
from jax.experimental.pallas import tpu as pltpu

F32 = jnp.float32
BF16 = jnp.bfloat16

EPS = 1e-6
CHUNK = 64
D_MODEL = 1024

SSD_D_INNER = 1024
SSD_HEAD_DIM = 64
SSD_N_HEADS = 16
SSD_N_GROUPS = 4
SSD_HEADS_PER_GROUP = 4
SSD_D_STATE = 128
SSD_CONV = 4
SSD_XBC = 2048
SSD_GROUP_W = SSD_HEADS_PER_GROUP * SSD_HEAD_DIM

CONV_D = 1024
CONV_K = 31
CONV_HALO = 32

MLA_HEADS = 8
MLA_Q_RANK = 384
MLA_Q_RANK_PAD = 512
MLA_KV_RANK = 256
MLA_NOPE = 128
MLA_ROPE = 64
MLA_V = 128
MLA_QK = MLA_NOPE + MLA_ROPE
MLA_QK_PAD = 256
ROPE_THETA = 10000.0

X_HEADS = 4
X_HEAD_DIM = 256
FFN_HIDDEN = 2816

LANE = 128
SUBLANE = 8

OFF_XBC = 0
OFF_GLU_A = 2048
OFF_GLU_G = 3072
OFF_GATE = 4096
OFF_Z = 7168
OFF_QLAT = 8192
OFF_CKV = 8704
OFF_MISC = 8960
OFF_KROT = 9088
PROJ_W = 9216
MISC_DT = 64
TAIL_W = PROJ_W - OFF_MISC
TAIL_MISC = 0
TAIL_KROT = 1

VMEM_LIMIT = 56 * 1024 * 1024


def _cparams(sem):
    return pltpu.CompilerParams(dimension_semantics=sem, vmem_limit_bytes=VMEM_LIMIT)


def _dot(a, b):
    return jnp.dot(a, b, preferred_element_type=F32)


def _dot_nt(a, b):
    return lax.dot_general(a, b, (((1,), (1,)), ((), ())), preferred_element_type=F32)


def _split3(x):
    hi = x.astype(BF16)
    r = x - hi.astype(F32)
    mid = r.astype(BF16)
    lo = (r - mid.astype(F32)).astype(BF16)
    return hi, mid, lo


def _dot_exact_rhs01(x, m01):
    hi, mid, lo = _split3(x)
    return _dot(hi, m01) + _dot(mid, m01) + _dot(lo, m01)


def _dot_exact_lhs01(m01, x):
    hi, mid, lo = _split3(x)
    return _dot(m01, hi) + _dot(m01, mid) + _dot(m01, lo)


def _sigmoid(x):
    return 1.0 / (1.0 + jnp.exp(-x))


def _silu(x):
    return x * _sigmoid(x)


def _in_proj_kernel(x_ref, g_ref, w_ref, o_ref, tail_ref, u_ref):
    j = pl.program_id(1)

    @pl.when(j == 0)
    def _():
        x = x_ref[...]
        ms = jnp.mean(x * x, axis=-1, keepdims=True)
        u_ref[...] = (x * lax.rsqrt(ms + EPS) * g_ref[...]).astype(BF16)

    acc = _dot(u_ref[...], w_ref[...])
    o_ref[...] = acc.astype(o_ref.dtype)

    @pl.when(j == pl.num_programs(1) - 1)
    def _():
        tail_ref[...] = acc[:, acc.shape[1] - TAIL_W:]


def _in_proj(x2, g, w, tm, tn):
    t, d = x2.shape
    n = w.shape[1]
    return pl.pallas_call(
        _in_proj_kernel,
        grid=(t // tm, n // tn),
        in_specs=[
            pl.BlockSpec((tm, d), lambda i, j: (i, 0)),
            pl.BlockSpec((1, d), lambda i, j: (0, 0)),
            pl.BlockSpec((d, tn), lambda i, j: (0, j)),
        ],
        out_specs=[pl.BlockSpec((tm, tn), lambda i, j: (i, j)), pl.BlockSpec((tm, TAIL_W), lambda i, j: (i, 0))],
        out_shape=[jax.ShapeDtypeStruct((t, n), BF16), jax.ShapeDtypeStruct((t, TAIL_W), F32)],
        scratch_shapes=[pltpu.VMEM((tm, d), BF16)],
        compiler_params=_cparams(("parallel", "arbitrary")),
        name="in_proj",
    )(x2, g, w)


def _ssd_kernel(xbc_ref, z_ref, misc_ref, convw_ref, convb_ref, dtb_ref, alog_s_ref,
                dskip_ref, normg_ref, expand_ref, tri_ref, y_ref, xpad_ref, state_ref, *, L, n_chunks):
    ts = L * n_chunks
    pad = SUBLANE

    @pl.when(pl.program_id(1) == 0)
    def _():
        xpad_ref[0:pad, :] = jnp.zeros((pad, SSD_XBC), F32)
        state_ref[...] = jnp.zeros(state_ref.shape, F32)

    xpad_ref[pad:pad + ts, :] = xbc_ref[0].astype(F32)
    conv = jnp.broadcast_to(convb_ref[...], (ts, SSD_XBC))
    for k in range(SSD_CONV):
        conv = conv + convw_ref[k:k + 1, :] * xpad_ref[pl.ds(pad - (SSD_CONV - 1) + k, ts), :]
    xpad_ref[0:pad, :] = xpad_ref[ts:ts + pad, :]
    xc = _silu(conv)

    misc = misc_ref[0]
    v = misc + dtb_ref[...]
    dt_s = jnp.maximum(v, 0.0) + jnp.log1p(jnp.exp(-jnp.abs(v)))
    a_neg_s = -jnp.exp(alog_s_ref[...])
    expand = expand_ref[...]
    tri = tri_ref[...]
    dt_e = _dot_exact_rhs01(dt_s, expand)
    a_s = dt_s * a_neg_s

    row_i = lax.broadcasted_iota(jnp.int32, (L, L), 0)
    col_i = lax.broadcasted_iota(jnp.int32, (L, L), 1)
    causal = row_i >= col_i
    lane_g = lax.broadcasted_iota(jnp.int32, (1, SSD_GROUP_W), 1) // SSD_HEAD_DIM

    z = z_ref[0].astype(F32)
    for c in range(n_chunks):
        r0 = c * L
        xs = xc[r0:r0 + L, 0:SSD_D_INNER]
        acs_s = _dot_exact_lhs01(tri, a_s[r0:r0 + L])
        acs_e = _dot_exact_rhs01(acs_s, expand)
        acs_st = acs_s.T
        last_e = acs_e[L - 1:L, :]
        exp_acs = jnp.exp(acs_e)
        to_end = jnp.exp(last_e - acs_e)
        chunk_decay = jnp.exp(last_e)
        xdt = xs * dt_e[r0:r0 + L]
        xdt_b = xdt.astype(BF16)
        xend_b = (xdt * to_end).astype(BF16)
        y_groups = []
        for g in range(SSD_N_GROUPS):
            gs = slice(g * SSD_GROUP_W, (g + 1) * SSD_GROUP_W)
            b_g = xc[r0:r0 + L, SSD_D_INNER + g * SSD_D_STATE:SSD_D_INNER + (g + 1) * SSD_D_STATE]
            c_g = xc[r0:r0 + L, SSD_D_INNER + (SSD_N_GROUPS + g) * SSD_D_STATE:
                     SSD_D_INNER + (SSD_N_GROUPS + g + 1) * SSD_D_STATE]
            b_gt = b_g.T.astype(BF16)
            c_gb = c_g.astype(BF16)
            cb = _dot(c_gb, b_gt)
            st = state_ref[g]
            y_g = _dot(c_gb, st.astype(BF16)) * exp_acs[:, gs]
            x_g = xdt_b[:, gs]
            for r in range(SSD_HEADS_PER_GROUP):
                h = g * SSD_HEADS_PER_GROUP + r
                col = acs_s[:, MISC_DT + h:MISC_DT + h + 1]
                row = acs_st[MISC_DT + h:MISC_DT + h + 1, :]
                decay = jnp.exp(jnp.where(causal, col - row, -jnp.inf))
                gm = (cb * decay).astype(BF16)
                x_h = jnp.where(lane_g == r, x_g, jnp.zeros_like(x_g))
                y_g = y_g + _dot(gm, x_h)
            state_ref[g] = st * chunk_decay[:, gs] + _dot(b_gt, xend_b[:, gs])
            y_groups.append(y_g)
        y = jnp.concatenate(y_groups, axis=-1) + dskip_ref[...] * xs
        y = y * _silu(z[r0:r0 + L])
        outs = []
        for g in range(SSD_N_GROUPS):
            yg = y[:, g * SSD_GROUP_W:(g + 1) * SSD_GROUP_W]
            outs.append(yg * lax.rsqrt(jnp.mean(yg * yg, axis=-1, keepdims=True) + EPS))
        y = jnp.concatenate(outs, axis=-1) * normg_ref[...]
        y_ref[0, r0:r0 + L, :] = y.astype(y_ref.dtype)


def _ssd(proj3, tail3, convw, convb, dtb, alog_s, dskip_e, normg, expand, tri, L, n_chunks):
    b, s, _ = proj3.shape
    ts = L * n_chunks
    full = lambda shape: pl.BlockSpec(shape, lambda bi, si: (0,) * len(shape))
    return pl.pallas_call(
        functools.partial(_ssd_kernel, L=L, n_chunks=n_chunks),
        grid=(b, s // ts),
        in_specs=[
            pl.BlockSpec((1, ts, SSD_XBC), lambda bi, si: (bi, si, OFF_XBC // SSD_XBC)),
            pl.BlockSpec((1, ts, SSD_D_INNER), lambda bi, si: (bi, si, OFF_Z // SSD_D_INNER)),
            pl.BlockSpec((1, ts, LANE), lambda bi, si: (bi, si, TAIL_MISC)),
            full((SSD_CONV, SSD_XBC)), full((1, SSD_XBC)), full((1, LANE)), full((1, LANE)),
            full((1, SSD_D_INNER)), full((1, SSD_D_INNER)),
            full((LANE, SSD_D_INNER)), full((L, L)),
        ],
        out_specs=pl.BlockSpec((1, ts, SSD_D_INNER), lambda bi, si: (bi, si, 0)),
        out_shape=jax.ShapeDtypeStruct((b, s, SSD_D_INNER), BF16),
        scratch_shapes=[
            pltpu.VMEM((SUBLANE + ts, SSD_XBC), F32),
            pltpu.VMEM((SSD_N_GROUPS, SSD_D_STATE, SSD_GROUP_W), F32),
        ],
        compiler_params=_cparams(("parallel", "arbitrary")),
        name="ssd",
    )(proj3, proj3, tail3, convw, convb, dtb, alog_s, dskip_e, normg, expand, tri)


def _conv_kernel(a_ref, g_ref, w_ref, b_ref, lng_ref, lnb_ref, o_ref, vpad_ref, sh_ref, acc_ref, *, tc, rb):
    n = CONV_HALO + tc
    n_cb = CONV_D // LANE

    @pl.when(pl.program_id(1) == 0)
    def _():
        vpad_ref[:, 0:CONV_HALO, :] = jnp.zeros((n_cb, CONV_HALO, LANE), F32)

    v = a_ref[0].astype(F32) * _sigmoid(g_ref[0].astype(F32))
    first = CONV_HALO - (CONV_K - 1)
    for cb in range(n_cb):
        ls = slice(cb * LANE, (cb + 1) * LANE)
        vpad_ref[cb, CONV_HALO:n, :] = v[:, ls]
        vp = vpad_ref[cb, 0:n, :]
        for r in range(1, SUBLANE):
            sh_ref[r - 1, cb, 0:n, :] = pltpu.roll(vp, n - r, axis=0)

        def row_block(i, carry, cb=cb, ls=ls):
            r0 = pl.multiple_of(i * rb, rb)
            acc = jnp.broadcast_to(b_ref[:, ls], (rb, LANE))
            for j in range(CONV_K):
                q, r = divmod(first + j, SUBLANE)
                start = pl.multiple_of(r0 + q * SUBLANE, SUBLANE)
                win = vpad_ref[cb, pl.ds(start, rb), :] if r == 0 else sh_ref[r - 1, cb, pl.ds(start, rb), :]
                acc = acc + w_ref[j:j + 1, ls] * win
            acc_ref[cb, pl.ds(r0, rb), :] = acc
            return carry

        lax.fori_loop(0, tc // rb, row_block, 0)
        vpad_ref[cb, 0:CONV_HALO, :] = vpad_ref[cb, tc:n, :]
    v = jnp.concatenate([acc_ref[cb] for cb in range(n_cb)], axis=-1)
    mu = jnp.mean(v, axis=-1, keepdims=True)
    vc = v - mu
    var = jnp.mean(vc * vc, axis=-1, keepdims=True)
    y = vc * lax.rsqrt(var + EPS) * lng_ref[...] + lnb_ref[...]
    o_ref[0] = _silu(y).astype(o_ref.dtype)


def _conformer_conv(proj3, w, bias, lng, lnb, tc, rb):
    b, s, _ = proj3.shape
    full = lambda shape: pl.BlockSpec(shape, lambda bi, si: (0,) * len(shape))
    return pl.pallas_call(
        functools.partial(_conv_kernel, tc=tc, rb=rb),
        grid=(b, s // tc),
        in_specs=[
            pl.BlockSpec((1, tc, CONV_D), lambda bi, si: (bi, si, OFF_GLU_A // CONV_D)),
            pl.BlockSpec((1, tc, CONV_D), lambda bi, si: (bi, si, OFF_GLU_G // CONV_D)),
            full((CONV_K, CONV_D)), full((1, CONV_D)), full((1, CONV_D)), full((1, CONV_D)),
        ],
        out_specs=pl.BlockSpec((1, tc, CONV_D), lambda bi, si: (bi, si, 0)),
        out_shape=jax.ShapeDtypeStruct((b, s, CONV_D), BF16),
        scratch_shapes=[pltpu.VMEM((CONV_D // LANE, CONV_HALO + tc + SUBLANE, LANE), F32),
                        pltpu.VMEM((SUBLANE - 1, CONV_D // LANE, CONV_HALO + tc + SUBLANE, LANE), F32),
                        pltpu.VMEM((CONV_D // LANE, tc, LANE), F32)],
        compiler_params=_cparams(("parallel", "arbitrary")),
        name="conformer_conv",
    )(proj3, proj3, w, bias, lng, lnb)


def _mla_prep_kernel(qlat_ref, ckv_ref, misc_ref, krot_ref, cos_ref, sin_ref, qag_ref, wq_ref, kvag_ref,
                     wkv_ref, qg_nope_ref, qg_rope_ref, qg_rot_ref, kg_nope_ref, kg_rope_ref, kg_rot_ref,
                     q_ref, k_ref, v_ref):
    hn = MLA_HEADS * MLA_NOPE
    cos2 = cos_ref[0]
    sin2 = sin_ref[0]
    scale = MLA_QK ** -0.5 * math.log2(math.e)

    ql = qlat_ref[0].astype(F32)
    ms = jnp.sum(ql * ql, axis=-1, keepdims=True) * (1.0 / MLA_Q_RANK)
    qa = (ql * lax.rsqrt(ms + EPS) * qag_ref[...]).astype(BF16)
    qf = _dot(qa, wq_ref[...])
    for h in range(MLA_HEADS):
        qn = qf[:, h * LANE:(h + 1) * LANE]
        qn = qn * lax.rsqrt(jnp.mean(qn * qn, axis=-1, keepdims=True) + EPS) * qg_nope_ref[...]
        qr = qf[:, hn + h * LANE:hn + (h + 1) * LANE]
        qs = qf[:, 2 * hn + h * LANE:2 * hn + (h + 1) * LANE]
        inv = lax.rsqrt(jnp.sum(qr * qr, axis=-1, keepdims=True) * (1.0 / MLA_ROPE) + EPS)
        qro = (qr * qg_rope_ref[...] * cos2 + qs * qg_rot_ref[...] * sin2) * inv
        q_ref[0, :, h * MLA_QK_PAD:h * MLA_QK_PAD + LANE] = (qn * scale).astype(q_ref.dtype)
        q_ref[0, :, h * MLA_QK_PAD + LANE:(h + 1) * MLA_QK_PAD] = (qro * scale).astype(q_ref.dtype)

    ckv = ckv_ref[0].astype(F32)
    ca = (ckv * lax.rsqrt(jnp.mean(ckv * ckv, axis=-1, keepdims=True) + EPS) * kvag_ref[...]).astype(BF16)
    kvf = _dot(ca, wkv_ref[...])
    lane = lax.broadcasted_iota(jnp.int32, (1, LANE), 1)
    kr = jnp.where(lane < MLA_ROPE, misc_ref[0], 0.0)
    ks = krot_ref[0]
    inv = lax.rsqrt(jnp.sum(kr * kr, axis=-1, keepdims=True) * (1.0 / MLA_ROPE) + EPS)
    kro_t = ((kr * kg_rope_ref[...] * cos2 + ks * kg_rot_ref[...] * sin2) * inv).T.astype(k_ref.dtype)
    for h in range(MLA_HEADS):
        kn = kvf[:, h * LANE:(h + 1) * LANE]
        kn = kn * lax.rsqrt(jnp.mean(kn * kn, axis=-1, keepdims=True) + EPS) * kg_nope_ref[...]
        k_ref[0, h, 0, 0:LANE, :] = kn.T.astype(k_ref.dtype)
        k_ref[0, h, 0, LANE:MLA_QK_PAD, :] = kro_t
    v_ref[0] = kvf[:, hn:].astype(v_ref.dtype)


def _mla_prep(proj3, tail3, cos2, sin2, qag, wq, kvag, wkv, qg_nope, qg_rope, qg_rot, kg_nope, kg_rope, kg_rot, tm):
    b, s, _ = proj3.shape
    full = lambda shape: pl.BlockSpec(shape, lambda bi, si: (0,) * len(shape))
    hq = MLA_HEADS * MLA_QK_PAD
    hv = MLA_HEADS * MLA_V
    return pl.pallas_call(
        _mla_prep_kernel,
        grid=(b, s // tm),
        in_specs=[
            pl.BlockSpec((1, tm, MLA_Q_RANK_PAD), lambda bi, si: (bi, si, OFF_QLAT // MLA_Q_RANK_PAD)),
            pl.BlockSpec((1, tm, MLA_KV_RANK), lambda bi, si: (bi, si, OFF_CKV // MLA_KV_RANK)),
            pl.BlockSpec((1, tm, LANE), lambda bi, si: (bi, si, TAIL_MISC)),
            pl.BlockSpec((1, tm, LANE), lambda bi, si: (bi, si, TAIL_KROT)),
            pl.BlockSpec((1, tm, LANE), lambda bi, si: (bi, si, 0)),
            pl.BlockSpec((1, tm, LANE), lambda bi, si: (bi, si, 0)),
            full((1, MLA_Q_RANK_PAD)), full(wq.shape), full((1, MLA_KV_RANK)), full(wkv.shape),
            full((1, LANE)), full((1, LANE)), full((1, LANE)), full((1, LANE)), full((1, LANE)), full((1, LANE)),
        ],
        out_specs=[
            pl.BlockSpec((1, tm, hq), lambda bi, si: (bi, si, 0)),
            pl.BlockSpec((1, MLA_HEADS, 1, MLA_QK_PAD, tm), lambda bi, si: (bi, 0, si, 0, 0)),
            pl.BlockSpec((1, tm, hv), lambda bi, si: (bi, si, 0)),
        ],
        out_shape=[
            jax.ShapeDtypeStruct((b, s, hq), BF16),
            jax.ShapeDtypeStruct((b, MLA_HEADS, s // tm, MLA_QK_PAD, tm), BF16),
            jax.ShapeDtypeStruct((b, s, hv), BF16),
        ],
        compiler_params=_cparams(("parallel", "parallel")),
        name="mla_prep",
    )(proj3, proj3, tail3, tail3, cos2, sin2, qag, wq, kvag, wkv, qg_nope, qg_rope, qg_rot, kg_nope, kg_rope,
      kg_rot)


def _attn_kernel(q_ref, kt_ref, v_ref, o_ref, sa_ref, sb_ref, m_ref, acc_ref, *, tq):
    i = pl.program_id(2)
    q = q_ref[0]
    reps = tq // LANE

    def scores(j, s_ref):
        s_ref[...] = _dot(q, kt_ref[0, 0, j])

    def consume(j, s_ref, masked):
        start = pl.multiple_of(j * tq, tq)
        v = v_ref[0, pl.ds(start, tq), :]
        v1 = jnp.concatenate([v, jnp.ones_like(v)], axis=-1)
        s = s_ref[...]
        if masked:
            rq = lax.broadcasted_iota(jnp.int32, (tq, tq), 0) // CHUNK
            ck = lax.broadcasted_iota(jnp.int32, (tq, tq), 1) // CHUNK
            s = jnp.where(ck <= rq, s, -jnp.inf)
        m = m_ref[...]
        m_new = jnp.maximum(m, jnp.max(s, axis=-1, keepdims=True))
        p = jnp.exp2(s - jnp.concatenate([m_new] * reps, axis=-1))
        alpha = jnp.exp2(m - m_new)
        acc_ref[...] = jnp.concatenate([alpha, alpha], axis=-1) * acc_ref[...] + _dot(p.astype(BF16), v1)
        m_ref[...] = m_new

    m_ref[...] = jnp.full(m_ref.shape, -jnp.inf, F32)
    acc_ref[...] = jnp.zeros(acc_ref.shape, F32)
    scores(0, sa_ref)

    def pair(jj, carry):
        j = 2 * jj
        scores(j + 1, sb_ref)
        consume(j, sa_ref, False)
        scores(j + 2, sa_ref)
        consume(j + 1, sb_ref, False)
        return carry

    lax.fori_loop(0, i // 2, pair, 0)

    @pl.when(i % 2 == 0)
    def _():
        consume(i, sa_ref, True)

    @pl.when(i % 2 == 1)
    def _():
        scores(i, sb_ref)
        consume(i - 1, sa_ref, False)
        consume(i, sb_ref, True)

    acc = acc_ref[...]
    o_ref[0] = (acc[:, :MLA_V] / acc[:, MLA_V:]).astype(o_ref.dtype)


def _attention(q, kt, v, tq):
    b, s, _ = q.shape
    n = s // tq
    return pl.pallas_call(
        functools.partial(_attn_kernel, tq=tq),
        grid=(b, MLA_HEADS, n),
        in_specs=[
            pl.BlockSpec((1, tq, MLA_QK_PAD), lambda bi, h, i: (bi, i, h)),
            pl.BlockSpec((1, 1, n, MLA_QK_PAD, tq), lambda bi, h, i: (bi, h, 0, 0, 0)),
            pl.BlockSpec((1, s, MLA_V), lambda bi, h, i: (bi, 0, h)),
        ],
        out_specs=pl.BlockSpec((1, tq, MLA_V), lambda bi, h, i: (bi, i, h)),
        out_shape=jax.ShapeDtypeStruct((b, s, MLA_HEADS * MLA_V), BF16),
        scratch_shapes=[pltpu.VMEM((tq, tq), F32), pltpu.VMEM((tq, tq), F32), pltpu.VMEM((tq, LANE), F32),
                        pltpu.VMEM((tq, 2 * MLA_V), F32)],
        compiler_params=_cparams(("parallel", "parallel", "arbitrary")),
        name="mla_attention",
    )(q, kt, v)


def _merge_kernel(x_ref, ssd_ref, conv_ref, att_ref, g0_ref, g1_ref, g2_ref, gb_ref, w_ssd_ref, w_conv_ref,
                  w_mla_ref, w_out_ref, o_ref):
    merged = _sigmoid(g0_ref[...].astype(F32) + gb_ref[0:1, :]) * _dot(ssd_ref[...], w_ssd_ref[...])
    merged = merged + _sigmoid(g1_ref[...].astype(F32) + gb_ref[1:2, :]) * _dot(conv_ref[...], w_conv_ref[...])
    merged = merged + _sigmoid(g2_ref[...].astype(F32) + gb_ref[2:3, :]) * _dot(att_ref[...], w_mla_ref[...])
    o_ref[...] = x_ref[...] + _dot(merged.astype(BF16), w_out_ref[...])


def _merge(x2, ssd_pre, conv_pre, att, proj2, gate_b, w_ssd, w_conv, w_mla, w_out, tm):
    t, d = x2.shape
    row = lambda c: pl.BlockSpec((tm, d), lambda i, c=c: (i, c))
    full = lambda shape: pl.BlockSpec(shape, lambda i: (0,) * len(shape))
    g0 = OFF_GATE // d
    return pl.pallas_call(
        _merge_kernel,
        grid=(t // tm,),
        in_specs=[row(0), row(0), row(0), row(0), row(g0), row(g0 + 1), row(g0 + 2), full(gate_b.shape),
                  full((d, d)), full((d, d)), full((d, d)), full((d, d))],
        out_specs=row(0),
        out_shape=jax.ShapeDtypeStruct((t, d), F32),
        compiler_params=_cparams(("parallel",)),
        name="merge",
    )(x2, ssd_pre, conv_pre, att, proj2, proj2, proj2, gate_b, w_ssd, w_conv, w_mla, w_out)


def _mem_kv_kernel(mem_ref, g_ref, w_ref, kg_ref, k_ref, v_ref):
    m = mem_ref[0]
    mn = (m * lax.rsqrt(jnp.mean(m * m, axis=-1, keepdims=True) + EPS) * g_ref[...]).astype(BF16)
    kv = _dot(mn, w_ref[...])
    for h in range(X_HEADS):
        kh = kv[:, h * X_HEAD_DIM:(h + 1) * X_HEAD_DIM]
        kh = kh * lax.rsqrt(jnp.mean(kh * kh, axis=-1, keepdims=True) + EPS) * kg_ref[...]
        k_ref[0, :, h * X_HEAD_DIM:(h + 1) * X_HEAD_DIM] = kh.astype(k_ref.dtype)
    v_ref[0] = kv[:, D_MODEL:].astype(v_ref.dtype)


def _mem_kv(mem, g, w_kv, kg):
    b, m, d = mem.shape
    full = lambda shape: pl.BlockSpec(shape, lambda bi: (0,) * len(shape))
    blk = pl.BlockSpec((1, m, d), lambda bi: (bi, 0, 0))
    return pl.pallas_call(
        _mem_kv_kernel,
        grid=(b,),
        in_specs=[blk, full((1, d)), full(w_kv.shape), full((1, X_HEAD_DIM))],
        out_specs=[blk, blk],
        out_shape=[jax.ShapeDtypeStruct((b, m, d), BF16), jax.ShapeDtypeStruct((b, m, d), BF16)],
        compiler_params=_cparams(("parallel",)),
        name="mem_kv",
    )(mem, g, w_kv, kg)


def _xattn_kernel(x_ref, k_ref, v_ref, g_ref, wq_ref, qg_ref, wo_ref, o_ref):
    x = x_ref[0]
    u = (x * lax.rsqrt(jnp.mean(x * x, axis=-1, keepdims=True) + EPS) * g_ref[...]).astype(BF16)
    q = _dot(u, wq_ref[...])
    scale = X_HEAD_DIM ** -0.5
    outs = []
    for h in range(X_HEADS):
        hs = slice(h * X_HEAD_DIM, (h + 1) * X_HEAD_DIM)
        qh = q[:, hs]
        qh = qh * lax.rsqrt(jnp.mean(qh * qh, axis=-1, keepdims=True) + EPS) * (qg_ref[...] * scale)
        s = _dot_nt(qh.astype(BF16), k_ref[0, :, hs])
        p = jnp.exp(s - jnp.max(s, axis=-1, keepdims=True))
        o = _dot(p.astype(BF16), v_ref[0, :, hs])
        outs.append((o / jnp.sum(p, axis=-1, keepdims=True)).astype(BF16))
    o_ref[0] = x + _dot(jnp.concatenate(outs, axis=-1), wo_ref[...])


def _xattn(x3, k, v, g, wq, qg, wo, tm):
    b, s, d = x3.shape
    m = k.shape[1]
    full = lambda shape: pl.BlockSpec(shape, lambda bi, si: (0,) * len(shape))
    return pl.pallas_call(
        _xattn_kernel,
        grid=(b, s // tm),
        in_specs=[
            pl.BlockSpec((1, tm, d), lambda bi, si: (bi, si, 0)),
            pl.BlockSpec((1, m, d), lambda bi, si: (bi, 0, 0)),
            pl.BlockSpec((1, m, d), lambda bi, si: (bi, 0, 0)),
            full((1, d)), full((d, d)), full((1, X_HEAD_DIM)), full((d, d)),
        ],
        out_specs=pl.BlockSpec((1, tm, d), lambda bi, si: (bi, si, 0)),
        out_shape=jax.ShapeDtypeStruct((b, s, d), F32),
        compiler_params=_cparams(("parallel", "parallel")),
        name="mem_xattn",
    )(x3, k, v, g, wq, qg, wo)


def _ffn_kernel(x_ref, g_ref, wg_ref, wu_ref, wo_ref, o_ref, *, hc):
    x = x_ref[...]
    u = (x * lax.rsqrt(jnp.mean(x * x, axis=-1, keepdims=True) + EPS) * g_ref[...]).astype(BF16)
    acc = x
    for c in range(0, FFN_HIDDEN, hc):
        gate = _dot(u, wg_ref[:, c:c + hc])
        up = _dot(u, wu_ref[:, c:c + hc])
        acc = acc + _dot((_silu(gate) * up).astype(BF16), wo_ref[c:c + hc, :])
    o_ref[...] = acc


def _ffn(x2, g, wg, wu, wo, tm, hc):
    t, d = x2.shape
    full = lambda shape: pl.BlockSpec(shape, lambda i: (0,) * len(shape))
    return pl.pallas_call(
        functools.partial(_ffn_kernel, hc=hc),
        grid=(t // tm,),
        in_specs=[pl.BlockSpec((tm, d), lambda i: (i, 0)), full((1, d)), full(wg.shape), full(wu.shape),
                  full(wo.shape)],
        out_specs=pl.BlockSpec((tm, d), lambda i: (i, 0)),
        out_shape=jax.ShapeDtypeStruct((t, d), F32),
        compiler_params=_cparams(("parallel",)),
        name="ffn",
    )(x2, g, wg, wu, wo)


def _swap_halves(a):
    h = a.shape[-1] // 2
    return jnp.concatenate([a[..., h:], a[..., :h]], axis=-1)


def _pad_last(a, width):
    return jnp.pad(a, [(0, 0)] * (a.ndim - 1) + [(0, width - a.shape[-1])])


def _in_proj_weight(w):
    z, xbc, dt, glu, q_lat, kv_lat, gates = jnp.split(
        w, np.cumsum([SSD_D_INNER, SSD_XBC, SSD_N_HEADS, 2 * CONV_D, MLA_Q_RANK, MLA_KV_RANK + MLA_ROPE])
        .tolist(), axis=-1)
    c_kv, k_rope = kv_lat[:, :MLA_KV_RANK], kv_lat[:, MLA_KV_RANK:]
    misc = _pad_last(jnp.concatenate([k_rope, dt], axis=-1), LANE)
    krot = _pad_last(_swap_halves(k_rope), LANE)
    out = jnp.concatenate([xbc, glu, gates, z, _pad_last(q_lat, MLA_Q_RANK_PAD), c_kv, misc, krot], axis=-1)
    return _pad_last(out, PROJ_W).astype(BF16)


def _mla_q_weight(w):
    w = w.reshape(MLA_Q_RANK, MLA_HEADS, MLA_QK)
    nope = w[:, :, :MLA_NOPE].reshape(MLA_Q_RANK, -1)
    rope = w[:, :, MLA_NOPE:]
    rope_p = _pad_last(rope, LANE).reshape(MLA_Q_RANK, -1)
    rot_p = _pad_last(_swap_halves(rope), LANE).reshape(MLA_Q_RANK, -1)
    out = jnp.concatenate([nope, rope_p, rot_p], axis=-1)
    return jnp.pad(out, [(0, MLA_Q_RANK_PAD - MLA_Q_RANK), (0, 0)]).astype(BF16)


def _mla_kv_weight(w):
    w = w.reshape(MLA_KV_RANK, MLA_HEADS, MLA_NOPE + MLA_V)
    return jnp.concatenate([w[:, :, :MLA_NOPE].reshape(MLA_KV_RANK, -1),
                            w[:, :, MLA_NOPE:].reshape(MLA_KV_RANK, -1)], axis=-1).astype(BF16)


def _row(a, width=None):
    a = a.reshape(1, -1).astype(F32)
    return a if width is None else _pad_last(a, width)


def _pick(n, prefs):
    for p in prefs:
        if n % p == 0:
            return p
    return n


def kernel(x, mem, positions, mix_norm_g, w_in, ssd_conv_w, ssd_conv_b, ssd_dt_bias, ssd_a_log, ssd_d, ssd_norm_g, ssd_w_out, conv_dw_w, conv_dw_b, conv_ln_g, conv_ln_b, conv_w_out, mla_q_a_g, mla_w_q_b, mla_kv_a_g, mla_w_kv_b, mla_q_norm_g, mla_k_norm_g, mla_w_o, gate_b, w_out, xattn_norm_g, mem_norm_g, xattn_w_q, xattn_w_kv, xattn_q_norm_g, xattn_k_norm_g, xattn_w_o, ffn_norm_g, ffn_w_in, ffn_w_out):
    b, s, d = x.shape
    t = b * s
    depth = w_in.shape[0]

    inv = ROPE_THETA ** (-jnp.arange(0, MLA_ROPE, 2, dtype=F32) / MLA_ROPE)
    ang = positions.astype(F32)[..., None] * inv
    cos, sin = jnp.cos(ang), jnp.sin(ang)
    cos2 = _pad_last(jnp.concatenate([cos, cos], axis=-1), LANE)
    sin2 = _pad_last(jnp.concatenate([-sin, sin], axis=-1), LANE)

    ssd_l = _pick(s, (128, 64))
    ssd_nc = 2 if s % (2 * ssd_l) == 0 else 1
    head_of_lane = np.arange(SSD_D_INNER) // SSD_HEAD_DIM
    expand = jnp.asarray((np.arange(LANE)[:, None] - MISC_DT) == head_of_lane[None, :], dtype=BF16)
    tri = jnp.asarray(np.tril(np.ones((ssd_l, ssd_l), np.float32)), dtype=BF16)

    tm_proj = _pick(t, (1024, 512, 256))
    tm_tok = _pick(t, (512, 256))
    tc = _pick(s, (256, 128))
    tq = _pick(s, (512, 256, 128))

    x2 = x.reshape(t, d)
    for l in range(depth):
        proj, tail = _in_proj(x2, _row(mix_norm_g[l]), _in_proj_weight(w_in[l]), tm_proj, 3072)
        proj3 = proj.reshape(b, s, PROJ_W)
        tail3 = tail.reshape(b, s, TAIL_W)

        dtb = jnp.pad(_row(ssd_dt_bias[l]), [(0, 0), (MISC_DT, LANE - MISC_DT - SSD_N_HEADS)])
        alog_s = jnp.pad(_row(ssd_a_log[l]), [(0, 0), (MISC_DT, LANE - MISC_DT - SSD_N_HEADS)])
        dskip_e = jnp.repeat(_row(ssd_d[l]), SSD_HEAD_DIM, axis=-1)
        ssd_pre = _ssd(proj3, tail3, ssd_conv_w[l], _row(ssd_conv_b[l]), dtb, alog_s, dskip_e,
                       _row(ssd_norm_g[l]), expand, tri, ssd_l, ssd_nc)

        conv_pre = _conformer_conv(proj3, conv_dw_w[l], _row(conv_dw_b[l]), _row(conv_ln_g[l]),
                                   _row(conv_ln_b[l]), tc, 64)

        qg, kg = mla_q_norm_g[l], mla_k_norm_g[l]
        q, k, v = _mla_prep(
            proj3, tail3, cos2, sin2, _row(mla_q_a_g[l], MLA_Q_RANK_PAD), _mla_q_weight(mla_w_q_b[l]),
            _row(mla_kv_a_g[l]), _mla_kv_weight(mla_w_kv_b[l]),
            _row(qg[:MLA_NOPE]), _row(qg[MLA_NOPE:], LANE), _row(_swap_halves(qg[MLA_NOPE:]), LANE),
            _row(kg[:MLA_NOPE]), _row(kg[MLA_NOPE:], LANE), _row(_swap_halves(kg[MLA_NOPE:]), LANE), tq)
        att = _attention(q, k, v, tq)

        x2 = _merge(x2, ssd_pre.reshape(t, d), conv_pre.reshape(t, d), att.reshape(t, d), proj,
                    gate_b[l].astype(F32), ssd_w_out[l].astype(BF16), conv_w_out[l].astype(BF16),
                    mla_w_o[l].astype(BF16), w_out[l].astype(BF16), tm_tok)

        mk, mv = _mem_kv(mem, _row(mem_norm_g[l]), xattn_w_kv[l].astype(BF16), _row(xattn_k_norm_g[l]))
        x2 = _xattn(x2.reshape(b, s, d), mk, mv, _row(xattn_norm_g[l]), xattn_w_q[l].astype(BF16),
                    _row(xattn_q_norm_g[l]), xattn_w_o[l].astype(BF16), tm_tok).reshape(t, d)

        w_ffn = ffn_w_in[l].astype(BF16)
        x2 = _ffn(x2, _row(ffn_norm_g[l]), w_ffn[:, :FFN_HIDDEN], w_ffn[:, FFN_HIDDEN:],
                  ffn_w_out[l].astype(BF16), tm_tok, 256)
    return x2.reshape(b, s, d)
```

```python
import functools
import math

import jax
import jax.numpy as jnp
import numpy as np
from jax import lax
from jax.experimental import pallas as pl
from jax.experimental.pallas import tpu as pltpu

F32 = jnp.float32
BF16 = jnp.bfloat16

EPS = 1e-6
CHUNK = 64
D_MODEL = 1024

SSD_D_INNER = 1024
SSD_HEAD_DIM = 64
SSD_N_HEADS = 16
SSD_N_GROUPS = 4
SSD_HEADS_PER_GROUP = 4
SSD_D_STATE = 128
SSD_CONV = 4
SSD_XBC = 2048
SSD_GROUP_W = SSD_HEADS_PER_GROUP * SSD_HEAD_DIM

CONV_D = 1024
CONV_K = 31
CONV_HALO = 32

MLA_HEADS = 8
MLA_Q_RANK = 384
MLA_Q_RANK_PAD = 512
MLA_KV_RANK = 256
MLA_NOPE = 128
MLA_ROPE = 64
MLA_V = 128
MLA_QK = MLA_NOPE + MLA_ROPE
MLA_QK_PAD = 256
ROPE_THETA = 10000.0

X_HEADS = 4
X_HEAD_DIM = 256
FFN_HIDDEN = 2816

LANE = 128
SUBLANE = 8

OFF_GLU_A = 0
OFF_GLU_G = 1024
OFF_XBC = 2048
OFF_GATE = 4096
OFF_Z = 7168
OFF_QLAT = 8192
OFF_CKV = 8704
OFF_MISC = 8960
OFF_KROT = 9088
PROJ_W = 9216
MISC_DT = 64
TAIL_W = PROJ_W - OFF_MISC
TAIL_MISC = 0
TAIL_KROT = 1

VMEM_LIMIT = 56 * 1024 * 1024


def _cparams(sem):
    return pltpu.CompilerParams(dimension_semantics=sem, vmem_limit_bytes=VMEM_LIMIT)


def _dot(a, b):
    return jnp.dot(a, b, preferred_element_type=F32)


def _dot_nt(a, b):
    return lax.dot_general(a, b, (((1,), (1,)), ((), ())), preferred_element_type=F32)


def _split3(x):
    hi = x.astype(BF16)
    r = x - hi.astype(F32)
    mid = r.astype(BF16)
    lo = (r - mid.astype(F32)).astype(BF16)
    return hi, mid, lo


def _dot_exact_rhs01(x, m01):
    hi, mid, lo = _split3(x)
    return _dot(hi, m01) + _dot(mid, m01) + _dot(lo, m01)


def _dot_exact_lhs01(m01, x):
    hi, mid, lo = _split3(x)
    return _dot(m01, hi) + _dot(m01, mid) + _dot(m01, lo)


def _sigmoid(x):
    return 1.0 / (1.0 + jnp.exp(-x))


def _silu(x):
    return x * _sigmoid(x)


CONV_LB = CONV_D // LANE
CONV_LB_PER_STEP = 3
CONV_RB = 64
PROJ_CHUNK = TAIL_W


def _in_proj_conv_kernel(x_ref, g_ref, w_ref, cw_ref, cbias_ref, lng_ref, lnb_ref, o_ref, tail_ref, conv_ref,
                         u_ref, vpad_ref, sh_ref, cacc_ref, *, tm, blocks_per_seq):
    i = pl.program_id(0)
    j = pl.program_id(1)
    nj = pl.num_programs(1)
    n = CONV_HALO + tm
    first = CONV_HALO - (CONV_K - 1)

    def project():
        acc = _dot(u_ref[...], w_ref[...])
        o_ref[...] = acc.astype(o_ref.dtype)
        return acc

    @pl.when((j == 0) & (i % blocks_per_seq == 0))
    def _():
        vpad_ref[:, 0:CONV_HALO, :] = jnp.zeros((CONV_LB, CONV_HALO, LANE), F32)

    @pl.when((j == 0) & (i % blocks_per_seq != 0))
    def _():
        vpad_ref[:, 0:CONV_HALO, :] = vpad_ref[:, tm:n, :]

    @pl.when(j == 0)
    def _():
        x = x_ref[...]
        ms = jnp.mean(x * x, axis=-1, keepdims=True)
        u_ref[...] = (x * lax.rsqrt(ms + EPS) * g_ref[...]).astype(BF16)
        acc = project()
        v = acc[:, OFF_GLU_A:OFF_GLU_A + CONV_D] * _sigmoid(acc[:, OFF_GLU_G:OFF_GLU_G + CONV_D])
        for cb in range(CONV_LB):
            vpad_ref[cb, CONV_HALO:n, :] = v[:, cb * LANE:(cb + 1) * LANE]

    def shift_task(cb):
        def run():
            vp = vpad_ref[cb, 0:n, :]
            for r in range(1, SUBLANE):
                sh_ref[r - 1, 0:n, :] = pltpu.roll(vp, n - r, axis=0)
        return run

    def tap_task(cb, r0):
        def run():
            cacc = jnp.broadcast_to(cbias_ref[cb], (CONV_RB, LANE))
            for tap in range(CONV_K):
                q, r = divmod(first + tap, SUBLANE)
                start = r0 + q * SUBLANE
                win = (vpad_ref[cb, start:start + CONV_RB, :] if r == 0
                       else sh_ref[r - 1, start:start + CONV_RB, :])
                cacc = cacc + cw_ref[cb, tap:tap + 1, :] * win
            cacc_ref[cb, r0:r0 + CONV_RB, :] = cacc
        return run

    @pl.when(j > 0)
    def _():
        tasks = []
        for c in range(CONV_LB_PER_STEP):
            cb = jnp.minimum((j - 1) * CONV_LB_PER_STEP + c, CONV_LB - 1)
            tasks.append(shift_task(cb))
            tasks.extend(tap_task(cb, r0) for r0 in range(0, tm, CONV_RB))
        tn = o_ref.shape[1]
        n_chunks = tn // PROJ_CHUNK
        per_chunk = -(-len(tasks) // n_chunks)
        u = u_ref[...]
        for k in range(n_chunks):
            cols = slice(k * PROJ_CHUNK, (k + 1) * PROJ_CHUNK)
            acc = _dot(u, w_ref[:, cols])
            o_ref[:, cols] = acc.astype(o_ref.dtype)
            for task in tasks[k * per_chunk:(k + 1) * per_chunk]:
                task()

        @pl.when(j == nj - 1)
        def _():
            tail_ref[...] = acc

    @pl.when(j == nj - 1)
    def _():
        v = jnp.concatenate([cacc_ref[cb] for cb in range(CONV_LB)], axis=-1)
        mu = jnp.mean(v, axis=-1, keepdims=True)
        vc = v - mu
        var = jnp.mean(vc * vc, axis=-1, keepdims=True)
        y = vc * lax.rsqrt(var + EPS) * lng_ref[...] + lnb_ref[...]
        conv_ref[...] = _silu(y).astype(conv_ref.dtype)


def _in_proj_conv(x2, g, w, cw, cbias, lng, lnb, tm, tn, blocks_per_seq):
    t, d = x2.shape
    n = w.shape[1]
    assert (n // tn - 1) * CONV_LB_PER_STEP >= CONV_LB and OFF_GLU_G + CONV_D <= tn
    full = lambda shape: pl.BlockSpec(shape, lambda i, j: (0,) * len(shape))
    rows = CONV_HALO + tm + SUBLANE
    return pl.pallas_call(
        functools.partial(_in_proj_conv_kernel, tm=tm, blocks_per_seq=blocks_per_seq),
        grid=(t // tm, n // tn),
        in_specs=[
            pl.BlockSpec((tm, d), lambda i, j: (i, 0)),
            full((1, d)),
            pl.BlockSpec((d, tn), lambda i, j: (0, j)),
            full((CONV_LB, CONV_K, LANE)), full((CONV_LB, 1, LANE)), full((1, CONV_D)), full((1, CONV_D)),
        ],
        out_specs=[pl.BlockSpec((tm, tn), lambda i, j: (i, j)), pl.BlockSpec((tm, TAIL_W), lambda i, j: (i, 0)),
                   pl.BlockSpec((tm, CONV_D), lambda i, j: (i, 0))],
        out_shape=[jax.ShapeDtypeStruct((t, n), BF16), jax.ShapeDtypeStruct((t, TAIL_W), F32),
                   jax.ShapeDtypeStruct((t, CONV_D), BF16)],
        scratch_shapes=[pltpu.VMEM((tm, d), BF16),
                        pltpu.VMEM((CONV_LB, rows, LANE), F32),
                        pltpu.VMEM((SUBLANE - 1, rows, LANE), F32),
                        pltpu.VMEM((CONV_LB, tm, LANE), F32)],
        compiler_params=_cparams(("arbitrary", "arbitrary")),
        name="in_proj_conv",
    )(x2, g, w, cw, cbias, lng, lnb)


def _ssd_kernel(xbc_ref, z_ref, misc_ref, convw_ref, convb_ref, dtb_ref, alog_s_ref,
                dskip_ref, normg_ref, expand_ref, tri_ref, y_ref, xpad_ref, state_ref, *, L, n_chunks):
    ts = L * n_chunks
    pad = SUBLANE

    @pl.when(pl.program_id(1) == 0)
    def _():
        xpad_ref[0:pad, :] = jnp.zeros((pad, SSD_XBC), F32)
        state_ref[...] = jnp.zeros(state_ref.shape, F32)

    xpad_ref[pad:pad + ts, :] = xbc_ref[0].astype(F32)
    conv = jnp.broadcast_to(convb_ref[...], (ts, SSD_XBC))
    for k in range(SSD_CONV):
        conv = conv + convw_ref[k:k + 1, :] * xpad_ref[pl.ds(pad - (SSD_CONV - 1) + k, ts), :]
    xpad_ref[0:pad, :] = xpad_ref[ts:ts + pad, :]
    xc = _silu(conv)

    misc = misc_ref[0]
    v = misc + dtb_ref[...]
    dt_s = jnp.maximum(v, 0.0) + jnp.log1p(jnp.exp(-jnp.abs(v)))
    a_neg_s = -jnp.exp(alog_s_ref[...])
    expand = expand_ref[...]
    tri = tri_ref[...]
    dt_e = _dot_exact_rhs01(dt_s, expand)
    a_s = dt_s * a_neg_s

    row_i = lax.broadcasted_iota(jnp.int32, (L, L), 0)
    col_i = lax.broadcasted_iota(jnp.int32, (L, L), 1)
    causal = row_i >= col_i
    lane_g = lax.broadcasted_iota(jnp.int32, (1, SSD_GROUP_W), 1) // SSD_HEAD_DIM

    z = z_ref[0].astype(F32)
    for c in range(n_chunks):
        r0 = c * L
        xs = xc[r0:r0 + L, 0:SSD_D_INNER]
        acs_s = _dot_exact_lhs01(tri, a_s[r0:r0 + L])
        acs_e = _dot_exact_rhs01(acs_s, expand)
        acs_st = acs_s.T
        last_e = acs_e[L - 1:L, :]
        exp_acs = jnp.exp(acs_e)
        to_end = jnp.exp(last_e - acs_e)
        chunk_decay = jnp.exp(last_e)
        xdt = xs * dt_e[r0:r0 + L]
        xdt_b = xdt.astype(BF16)
        xend_b = (xdt * to_end).astype(BF16)
        y_groups = []
        for g in range(SSD_N_GROUPS):
            gs = slice(g * SSD_GROUP_W, (g + 1) * SSD_GROUP_W)
            b_g = xc[r0:r0 + L, SSD_D_INNER + g * SSD_D_STATE:SSD_D_INNER + (g + 1) * SSD_D_STATE]
            c_g = xc[r0:r0 + L, SSD_D_INNER + (SSD_N_GROUPS + g) * SSD_D_STATE:
                     SSD_D_INNER + (SSD_N_GROUPS + g + 1) * SSD_D_STATE]
            b_gt = b_g.T.astype(BF16)
            c_gb = c_g.astype(BF16)
            cb = _dot(c_gb, b_gt)
            st = state_ref[g]
            y_g = _dot(c_gb, st.astype(BF16)) * exp_acs[:, gs]
            x_g = xdt_b[:, gs]
            for r in range(SSD_HEADS_PER_GROUP):
                h = g * SSD_HEADS_PER_GROUP + r
                col = acs_s[:, MISC_DT + h:MISC_DT + h + 1]
                row = acs_st[MISC_DT + h:MISC_DT + h + 1, :]
                decay = jnp.exp(jnp.where(causal, col - row, -jnp.inf))
                gm = (cb * decay).astype(BF16)
                x_h = jnp.where(lane_g == r, x_g, jnp.zeros_like(x_g))
                y_g = y_g + _dot(gm, x_h)
            state_ref[g] = st * chunk_decay[:, gs] + _dot(b_gt, xend_b[:, gs])
            y_groups.append(y_g)
        y = jnp.concatenate(y_groups, axis=-1) + dskip_ref[...] * xs
        y = y * _silu(z[r0:r0 + L])
        outs = []
        for g in range(SSD_N_GROUPS):
            yg = y[:, g * SSD_GROUP_W:(g + 1) * SSD_GROUP_W]
            outs.append(yg * lax.rsqrt(jnp.mean(yg * yg, axis=-1, keepdims=True) + EPS))
        y = jnp.concatenate(outs, axis=-1) * normg_ref[...]
        y_ref[0, r0:r0 + L, :] = y.astype(y_ref.dtype)


def _ssd(proj3, tail3, convw, convb, dtb, alog_s, dskip_e, normg, expand, tri, L, n_chunks):
    b, s, _ = proj3.shape
    ts = L * n_chunks
    full = lambda shape: pl.BlockSpec(shape, lambda bi, si: (0,) * len(shape))
    return pl.pallas_call(
        functools.partial(_ssd_kernel, L=L, n_chunks=n_chunks),
        grid=(b, s // ts),
        in_specs=[
            pl.BlockSpec((1, ts, SSD_XBC), lambda bi, si: (bi, si, OFF_XBC // SSD_XBC)),
            pl.BlockSpec((1, ts, SSD_D_INNER), lambda bi, si: (bi, si, OFF_Z // SSD_D_INNER)),
            pl.BlockSpec((1, ts, LANE), lambda bi, si: (bi, si, TAIL_MISC)),
            full((SSD_CONV, SSD_XBC)), full((1, SSD_XBC)), full((1, LANE)), full((1, LANE)),
            full((1, SSD_D_INNER)), full((1, SSD_D_INNER)),
            full((LANE, SSD_D_INNER)), full((L, L)),
        ],
        out_specs=pl.BlockSpec((1, ts, SSD_D_INNER), lambda bi, si: (bi, si, 0)),
        out_shape=jax.ShapeDtypeStruct((b, s, SSD_D_INNER), BF16),
        scratch_shapes=[
            pltpu.VMEM((SUBLANE + ts, SSD_XBC), F32),
            pltpu.VMEM((SSD_N_GROUPS, SSD_D_STATE, SSD_GROUP_W), F32),
        ],
        compiler_params=_cparams(("parallel", "arbitrary")),
        name="ssd",
    )(proj3, proj3, tail3, convw, convb, dtb, alog_s, dskip_e, normg, expand, tri)


def _mla_prep_kernel(qlat_ref, ckv_ref, misc_ref, krot_ref, cos_ref, sin_ref, qag_ref, wq_ref, kvag_ref,
                     wkv_ref, qg_nope_ref, qg_rope_ref, qg_rot_ref, kg_nope_ref, kg_rope_ref, kg_rot_ref,
                     q_ref, k_ref, v_ref):
    hn = MLA_HEADS * MLA_NOPE
    cos2 = cos_ref[0]
    sin2 = sin_ref[0]
    scale = MLA_QK ** -0.5 * math.log2(math.e)

    ql = qlat_ref[0].astype(F32)
    ms = jnp.sum(ql * ql, axis=-1, keepdims=True) * (1.0 / MLA_Q_RANK)
    qa = (ql * lax.rsqrt(ms + EPS) * qag_ref[...]).astype(BF16)
    qf = _dot(qa, wq_ref[...])
    for h in range(MLA_HEADS):
        qn = qf[:, h * LANE:(h + 1) * LANE]
        qn = qn * lax.rsqrt(jnp.mean(qn * qn, axis=-1, keepdims=True) + EPS) * qg_nope_ref[...]
        qr = qf[:, hn + h * LANE:hn + (h + 1) * LANE]
        qs = qf[:, 2 * hn + h * LANE:2 * hn + (h + 1) * LANE]
        inv = lax.rsqrt(jnp.sum(qr * qr, axis=-1, keepdims=True) * (1.0 / MLA_ROPE) + EPS)
        qro = (qr * qg_rope_ref[...] * cos2 + qs * qg_rot_ref[...] * sin2) * inv
        q_ref[0, :, h * MLA_QK_PAD:h * MLA_QK_PAD + LANE] = (qn * scale).astype(q_ref.dtype)
        q_ref[0, :, h * MLA_QK_PAD + LANE:(h + 1) * MLA_QK_PAD] = (qro * scale).astype(q_ref.dtype)

    ckv = ckv_ref[0].astype(F32)
    ca = (ckv * lax.rsqrt(jnp.mean(ckv * ckv, axis=-1, keepdims=True) + EPS) * kvag_ref[...]).astype(BF16)
    kvf = _dot(ca, wkv_ref[...])
    lane = lax.broadcasted_iota(jnp.int32, (1, LANE), 1)
    kr = jnp.where(lane < MLA_ROPE, misc_ref[0], 0.0)
    ks = krot_ref[0]
    inv = lax.rsqrt(jnp.sum(kr * kr, axis=-1, keepdims=True) * (1.0 / MLA_ROPE) + EPS)
    kro_t = ((kr * kg_rope_ref[...] * cos2 + ks * kg_rot_ref[...] * sin2) * inv).T.astype(k_ref.dtype)
    for h in range(MLA_HEADS):
        kn = kvf[:, h * LANE:(h + 1) * LANE]
        kn = kn * lax.rsqrt(jnp.mean(kn * kn, axis=-1, keepdims=True) + EPS) * kg_nope_ref[...]
        k_ref[0, h, 0, 0:LANE, :] = kn.T.astype(k_ref.dtype)
        k_ref[0, h, 0, LANE:MLA_QK_PAD, :] = kro_t
    v_ref[0] = kvf[:, hn:].astype(v_ref.dtype)


def _mla_prep(proj3, tail3, cos2, sin2, qag, wq, kvag, wkv, qg_nope, qg_rope, qg_rot, kg_nope, kg_rope, kg_rot, tm):
    b, s, _ = proj3.shape
    full = lambda shape: pl.BlockSpec(shape, lambda bi, si: (0,) * len(shape))
    hq = MLA_HEADS * MLA_QK_PAD
    hv = MLA_HEADS * MLA_V
    return pl.pallas_call(
        _mla_prep_kernel,
        grid=(b, s // tm),
        in_specs=[
            pl.BlockSpec((1, tm, MLA_Q_RANK_PAD), lambda bi, si: (bi, si, OFF_QLAT // MLA_Q_RANK_PAD)),
            pl.BlockSpec((1, tm, MLA_KV_RANK), lambda bi, si: (bi, si, OFF_CKV // MLA_KV_RANK)),
            pl.BlockSpec((1, tm, LANE), lambda bi, si: (bi, si, TAIL_MISC)),
            pl.BlockSpec((1, tm, LANE), lambda bi, si: (bi, si, TAIL_KROT)),
            pl.BlockSpec((1, tm, LANE), lambda bi, si: (bi, si, 0)),
            pl.BlockSpec((1, tm, LANE), lambda bi, si: (bi, si, 0)),
            full((1, MLA_Q_RANK_PAD)), full(wq.shape), full((1, MLA_KV_RANK)), full(wkv.shape),
            full((1, LANE)), full((1, LANE)), full((1, LANE)), full((1, LANE)), full((1, LANE)), full((1, LANE)),
        ],
        out_specs=[
            pl.BlockSpec((1, tm, hq), lambda bi, si: (bi, si, 0)),
            pl.BlockSpec((1, MLA_HEADS, 1, MLA_QK_PAD, tm), lambda bi, si: (bi, 0, si, 0, 0)),
            pl.BlockSpec((1, tm, hv), lambda bi, si: (bi, si, 0)),
        ],
        out_shape=[
            jax.ShapeDtypeStruct((b, s, hq), BF16),
            jax.ShapeDtypeStruct((b, MLA_HEADS, s // tm, MLA_QK_PAD, tm), BF16),
            jax.ShapeDtypeStruct((b, s, hv), BF16),
        ],
        compiler_params=_cparams(("parallel", "parallel")),
        name="mla_prep",
    )(proj3, proj3, tail3, tail3, cos2, sin2, qag, wq, kvag, wkv, qg_nope, qg_rope, qg_rot, kg_nope, kg_rope,
      kg_rot)


def _attn_kernel(q_ref, kt_ref, v_ref, o_ref, sa_ref, sb_ref, m_ref, acc_ref, *, tq):
    i = pl.program_id(2)
    q = q_ref[0]
    reps = tq // LANE

    def scores(j, s_ref):
        s_ref[...] = _dot(q, kt_ref[0, 0, j])

    def consume(j, s_ref, masked):
        start = pl.multiple_of(j * tq, tq)
        v = v_ref[0, pl.ds(start, tq), :]
        v1 = jnp.concatenate([v, jnp.ones_like(v)], axis=-1)
        s = s_ref[...]
        if masked:
            rq = lax.broadcasted_iota(jnp.int32, (tq, tq), 0) // CHUNK
            ck = lax.broadcasted_iota(jnp.int32, (tq, tq), 1) // CHUNK
            s = jnp.where(ck <= rq, s, -jnp.inf)
        m = m_ref[...]
        m_new = jnp.maximum(m, jnp.max(s, axis=-1, keepdims=True))
        p = jnp.exp2(s - jnp.concatenate([m_new] * reps, axis=-1))
        alpha = jnp.exp2(m - m_new)
        acc_ref[...] = jnp.concatenate([alpha, alpha], axis=-1) * acc_ref[...] + _dot(p.astype(BF16), v1)
        m_ref[...] = m_new

    m_ref[...] = jnp.full(m_ref.shape, -jnp.inf, F32)
    acc_ref[...] = jnp.zeros(acc_ref.shape, F32)
    scores(0, sa_ref)

    def pair(jj, carry):
        j = 2 * jj
        scores(j + 1, sb_ref)
        consume(j, sa_ref, False)
        scores(j + 2, sa_ref)
        consume(j + 1, sb_ref, False)
        return carry

    lax.fori_loop(0, i // 2, pair, 0)

    @pl.when(i % 2 == 0)
    def _():
        consume(i, sa_ref, True)

    @pl.when(i % 2 == 1)
    def _():
        scores(i, sb_ref)
        consume(i - 1, sa_ref, False)
        consume(i, sb_ref, True)

    acc = acc_ref[...]
    o_ref[0] = (acc[:, :MLA_V] / acc[:, MLA_V:]).astype(o_ref.dtype)


def _attention(q, kt, v, tq):
    b, s, _ = q.shape
    n = s // tq
    return pl.pallas_call(
        functools.partial(_attn_kernel, tq=tq),
        grid=(b, MLA_HEADS, n),
        in_specs=[
            pl.BlockSpec((1, tq, MLA_QK_PAD), lambda bi, h, i: (bi, i, h)),
            pl.BlockSpec((1, 1, n, MLA_QK_PAD, tq), lambda bi, h, i: (bi, h, 0, 0, 0)),
            pl.BlockSpec((1, s, MLA_V), lambda bi, h, i: (bi, 0, h)),
        ],
        out_specs=pl.BlockSpec((1, tq, MLA_V), lambda bi, h, i: (bi, i, h)),
        out_shape=jax.ShapeDtypeStruct((b, s, MLA_HEADS * MLA_V), BF16),
        scratch_shapes=[pltpu.VMEM((tq, tq), F32), pltpu.VMEM((tq, tq), F32), pltpu.VMEM((tq, LANE), F32),
                        pltpu.VMEM((tq, 2 * MLA_V), F32)],
        compiler_params=_cparams(("parallel", "parallel", "arbitrary")),
        name="mla_attention",
    )(q, kt, v)


def _merge_kernel(x_ref, ssd_ref, conv_ref, att_ref, g0_ref, g1_ref, g2_ref, gb_ref, w_ssd_ref, w_conv_ref,
                  w_mla_ref, w_out_ref, o_ref):
    merged = _sigmoid(g0_ref[...].astype(F32) + gb_ref[0:1, :]) * _dot(ssd_ref[...], w_ssd_ref[...])
    merged = merged + _sigmoid(g1_ref[...].astype(F32) + gb_ref[1:2, :]) * _dot(conv_ref[...], w_conv_ref[...])
    merged = merged + _sigmoid(g2_ref[...].astype(F32) + gb_ref[2:3, :]) * _dot(att_ref[...], w_mla_ref[...])
    o_ref[...] = x_ref[...] + _dot(merged.astype(BF16), w_out_ref[...])


def _merge(x2, ssd_pre, conv_pre, att, proj2, gate_b, w_ssd, w_conv, w_mla, w_out, tm):
    t, d = x2.shape
    row = lambda c: pl.BlockSpec((tm, d), lambda i, c=c: (i, c))
    full = lambda shape: pl.BlockSpec(shape, lambda i: (0,) * len(shape))
    g0 = OFF_GATE // d
    return pl.pallas_call(
        _merge_kernel,
        grid=(t // tm,),
        in_specs=[row(0), row(0), row(0), row(0), row(g0), row(g0 + 1), row(g0 + 2), full(gate_b.shape),
                  full((d, d)), full((d, d)), full((d, d)), full((d, d))],
        out_specs=row(0),
        out_shape=jax.ShapeDtypeStruct((t, d), F32),
        compiler_params=_cparams(("parallel",)),
        name="merge",
    )(x2, ssd_pre, conv_pre, att, proj2, proj2, proj2, gate_b, w_ssd, w_conv, w_mla, w_out)


def _mem_kv_kernel(mem_ref, g_ref, w_ref, kg_ref, k_ref, v_ref):
    m = mem_ref[0]
    mn = (m * lax.rsqrt(jnp.mean(m * m, axis=-1, keepdims=True) + EPS) * g_ref[...]).astype(BF16)
    kv = _dot(mn, w_ref[...])
    for h in range(X_HEADS):
        kh = kv[:, h * X_HEAD_DIM:(h + 1) * X_HEAD_DIM]
        kh = kh * lax.rsqrt(jnp.mean(kh * kh, axis=-1, keepdims=True) + EPS) * kg_ref[...]
        k_ref[0, :, h * X_HEAD_DIM:(h + 1) * X_HEAD_DIM] = kh.astype(k_ref.dtype)
    v_ref[0] = kv[:, D_MODEL:].astype(v_ref.dtype)


def _mem_kv(mem, g, w_kv, kg):
    b, m, d = mem.shape
    full = lambda shape: pl.BlockSpec(shape, lambda bi: (0,) * len(shape))
    blk = pl.BlockSpec((1, m, d), lambda bi: (bi, 0, 0))
    return pl.pallas_call(
        _mem_kv_kernel,
        grid=(b,),
        in_specs=[blk, full((1, d)), full(w_kv.shape), full((1, X_HEAD_DIM))],
        out_specs=[blk, blk],
        out_shape=[jax.ShapeDtypeStruct((b, m, d), BF16), jax.ShapeDtypeStruct((b, m, d), BF16)],
        compiler_params=_cparams(("parallel",)),
        name="mem_kv",
    )(mem, g, w_kv, kg)


def _xattn_kernel(x_ref, k_ref, v_ref, g_ref, wq_ref, qg_ref, wo_ref, o_ref):
    x = x_ref[0]
    u = (x * lax.rsqrt(jnp.mean(x * x, axis=-1, keepdims=True) + EPS) * g_ref[...]).astype(BF16)
    q = _dot(u, wq_ref[...])
    scale = X_HEAD_DIM ** -0.5
    outs = []
    for h in range(X_HEADS):
        hs = slice(h * X_HEAD_DIM, (h + 1) * X_HEAD_DIM)
        qh = q[:, hs]
        qh = qh * lax.rsqrt(jnp.mean(qh * qh, axis=-1, keepdims=True) + EPS) * (qg_ref[...] * scale)
        s = _dot_nt(qh.astype(BF16), k_ref[0, :, hs])
        p = jnp.exp(s - jnp.max(s, axis=-1, keepdims=True))
        o = _dot(p.astype(BF16), v_ref[0, :, hs])
        outs.append((o / jnp.sum(p, axis=-1, keepdims=True)).astype(BF16))
    o_ref[0] = x + _dot(jnp.concatenate(outs, axis=-1), wo_ref[...])


def _xattn(x3, k, v, g, wq, qg, wo, tm):
    b, s, d = x3.shape
    m = k.shape[1]
    full = lambda shape: pl.BlockSpec(shape, lambda bi, si: (0,) * len(shape))
    return pl.pallas_call(
        _xattn_kernel,
        grid=(b, s // tm),
        in_specs=[
            pl.BlockSpec((1, tm, d), lambda bi, si: (bi, si, 0)),
            pl.BlockSpec((1, m, d), lambda bi, si: (bi, 0, 0)),
            pl.BlockSpec((1, m, d), lambda bi, si: (bi, 0, 0)),
            full((1, d)), full((d, d)), full((1, X_HEAD_DIM)), full((d, d)),
        ],
        out_specs=pl.BlockSpec((1, tm, d), lambda bi, si: (bi, si, 0)),
        out_shape=jax.ShapeDtypeStruct((b, s, d), F32),
        compiler_params=_cparams(("parallel", "parallel")),
        name="mem_xattn",
    )(x3, k, v, g, wq, qg, wo)


def _ffn_kernel(x_ref, g_ref, wg_ref, wu_ref, wo_ref, o_ref, *, hc):
    x = x_ref[...]
    u = (x * lax.rsqrt(jnp.mean(x * x, axis=-1, keepdims=True) + EPS) * g_ref[...]).astype(BF16)
    acc = x
    for c in range(0, FFN_HIDDEN, hc):
        gate = _dot(u, wg_ref[:, c:c + hc])
        up = _dot(u, wu_ref[:, c:c + hc])
        acc = acc + _dot((_silu(gate) * up).astype(BF16), wo_ref[c:c + hc, :])
    o_ref[...] = acc


def _ffn(x2, g, wg, wu, wo, tm, hc):
    t, d = x2.shape
    full = lambda shape: pl.BlockSpec(shape, lambda i: (0,) * len(shape))
    return pl.pallas_call(
        functools.partial(_ffn_kernel, hc=hc),
        grid=(t // tm,),
        in_specs=[pl.BlockSpec((tm, d), lambda i: (i, 0)), full((1, d)), full(wg.shape), full(wu.shape),
                  full(wo.shape)],
        out_specs=pl.BlockSpec((tm, d), lambda i: (i, 0)),
        out_shape=jax.ShapeDtypeStruct((t, d), F32),
        compiler_params=_cparams(("parallel",)),
        name="ffn",
    )(x2, g, wg, wu, wo)


def _swap_halves(a):
    h = a.shape[-1] // 2
    return jnp.concatenate([a[..., h:], a[..., :h]], axis=-1)


def _pad_last(a, width):
    return jnp.pad(a, [(0, 0)] * (a.ndim - 1) + [(0, width - a.shape[-1])])


def _in_proj_weight(w):
    z, xbc, dt, glu, q_lat, kv_lat, gates = jnp.split(
        w, np.cumsum([SSD_D_INNER, SSD_XBC, SSD_N_HEADS, 2 * CONV_D, MLA_Q_RANK, MLA_KV_RANK + MLA_ROPE])
        .tolist(), axis=-1)
    c_kv, k_rope = kv_lat[:, :MLA_KV_RANK], kv_lat[:, MLA_KV_RANK:]
    misc = _pad_last(jnp.concatenate([k_rope, dt], axis=-1), LANE)
    krot = _pad_last(_swap_halves(k_rope), LANE)
    out = jnp.concatenate([glu, xbc, gates, z, _pad_last(q_lat, MLA_Q_RANK_PAD), c_kv, misc, krot], axis=-1)
    return _pad_last(out, PROJ_W).astype(BF16)


def _mla_q_weight(w):
    w = w.reshape(MLA_Q_RANK, MLA_HEADS, MLA_QK)
    nope = w[:, :, :MLA_NOPE].reshape(MLA_Q_RANK, -1)
    rope = w[:, :, MLA_NOPE:]
    rope_p = _pad_last(rope, LANE).reshape(MLA_Q_RANK, -1)
    rot_p = _pad_last(_swap_halves(rope), LANE).reshape(MLA_Q_RANK, -1)
    out = jnp.concatenate([nope, rope_p, rot_p], axis=-1)
    return jnp.pad(out, [(0, MLA_Q_RANK_PAD - MLA_Q_RANK), (0, 0)]).astype(BF16)


def _mla_kv_weight(w):
    w = w.reshape(MLA_KV_RANK, MLA_HEADS, MLA_NOPE + MLA_V)
    return jnp.concatenate([w[:, :, :MLA_NOPE].reshape(MLA_KV_RANK, -1),
                            w[:, :, MLA_NOPE:].reshape(MLA_KV_RANK, -1)], axis=-1).astype(BF16)


def _row(a, width=None):
    a = a.reshape(1, -1).astype(F32)
    return a if width is None else _pad_last(a, width)


def _pick(n, prefs):
    for p in prefs:
        if n % p == 0:
            return p
    return n


def kernel(x, mem, positions, mix_norm_g, w_in, ssd_conv_w, ssd_conv_b, ssd_dt_bias, ssd_a_log, ssd_d, ssd_norm_g, ssd_w_out, conv_dw_w, conv_dw_b, conv_ln_g, conv_ln_b, conv_w_out, mla_q_a_g, mla_w_q_b, mla_kv_a_g, mla_w_kv_b, mla_q_norm_g, mla_k_norm_g, mla_w_o, gate_b, w_out, xattn_norm_g, mem_norm_g, xattn_w_q, xattn_w_kv, xattn_q_norm_g, xattn_k_norm_g, xattn_w_o, ffn_norm_g, ffn_w_in, ffn_w_out):
    b, s, d = x.shape
    t = b * s
    depth = w_in.shape[0]

    inv = ROPE_THETA ** (-jnp.arange(0, MLA_ROPE, 2, dtype=F32) / MLA_ROPE)
    ang = positions.astype(F32)[..., None] * inv
    cos, sin = jnp.cos(ang), jnp.sin(ang)
    cos2 = _pad_last(jnp.concatenate([cos, cos], axis=-1), LANE)
    sin2 = _pad_last(jnp.concatenate([-sin, sin], axis=-1), LANE)

    ssd_l = _pick(s, (128, 64))
    ssd_nc = 2 if s % (2 * ssd_l) == 0 else 1
    head_of_lane = np.arange(SSD_D_INNER) // SSD_HEAD_DIM
    expand = jnp.asarray((np.arange(LANE)[:, None] - MISC_DT) == head_of_lane[None, :], dtype=BF16)
    tri = jnp.asarray(np.tril(np.ones((ssd_l, ssd_l), np.float32)), dtype=BF16)

    tm_proj = _pick(s, (1024, 512, 256))
    tm_tok = _pick(t, (512, 256))
    tq = _pick(s, (512, 256, 128))

    x2 = x.reshape(t, d)
    for l in range(depth):
        conv_w_lb = conv_dw_w[l].astype(F32).reshape(CONV_K, CONV_LB, LANE).transpose(1, 0, 2)
        proj, tail, conv_pre = _in_proj_conv(
            x2, _row(mix_norm_g[l]), _in_proj_weight(w_in[l]), conv_w_lb,
            conv_dw_b[l].astype(F32).reshape(CONV_LB, 1, LANE), _row(conv_ln_g[l]), _row(conv_ln_b[l]),
            tm_proj, 2304, s // tm_proj)
        proj3 = proj.reshape(b, s, PROJ_W)
        tail3 = tail.reshape(b, s, TAIL_W)

        dtb = jnp.pad(_row(ssd_dt_bias[l]), [(0, 0), (MISC_DT, LANE - MISC_DT - SSD_N_HEADS)])
        alog_s = jnp.pad(_row(ssd_a_log[l]), [(0, 0), (MISC_DT, LANE - MISC_DT - SSD_N_HEADS)])
        dskip_e = jnp.repeat(_row(ssd_d[l]), SSD_HEAD_DIM, axis=-1)
        ssd_pre = _ssd(proj3, tail3, ssd_conv_w[l], _row(ssd_conv_b[l]), dtb, alog_s, dskip_e,
                       _row(ssd_norm_g[l]), expand, tri, ssd_l, ssd_nc)

        qg, kg = mla_q_norm_g[l], mla_k_norm_g[l]
        q, k, v = _mla_prep(
            proj3, tail3, cos2, sin2, _row(mla_q_a_g[l], MLA_Q_RANK_PAD), _mla_q_weight(mla_w_q_b[l]),
            _row(mla_kv_a_g[l]), _mla_kv_weight(mla_w_kv_b[l]),
            _row(qg[:MLA_NOPE]), _row(qg[MLA_NOPE:], LANE), _row(_swap_halves(qg[MLA_NOPE:]), LANE),
            _row(kg[:MLA_NOPE]), _row(kg[MLA_NOPE:], LANE), _row(_swap_halves(kg[MLA_NOPE:]), LANE), tq)
        att = _attention(q, k, v, tq)

        x2 = _merge(x2, ssd_pre.reshape(t, d), conv_pre.reshape(t, d), att.reshape(t, d), proj,
                    gate_b[l].astype(F32), ssd_w_out[l].astype(BF16), conv_w_out[l].astype(BF16),
                    mla_w_o[l].astype(BF16), w_out[l].astype(BF16), tm_tok)

        mk, mv = _mem_kv(mem, _row(mem_norm_g[l]), xattn_w_kv[l].astype(BF16), _row(xattn_k_norm_g[l]))
        x2 = _xattn(x2.reshape(b, s, d), mk, mv, _row(xattn_norm_g[l]), xattn_w_q[l].astype(BF16),
                    _row(xattn_q_norm_g[l]), xattn_w_o[l].astype(BF16), tm_tok).reshape(t, d)

        w_ffn = ffn_w_in[l].astype(BF16)
        x2 = _ffn(x2, _row(ffn_norm_g[l]), w_ffn[:, :FFN_HIDDEN], w_ffn[:, FFN_HIDDEN:],
                  ffn_w_out[l].astype(BF16), tm_tok, 256)
    return x2.reshape(b, s, d)
```

```python
import functools
import math

import jax
import jax.numpy as jnp
import numpy as np
from jax import lax
from jax.experimental import pallas as pl
from jax.experimental.pallas import tpu as pltpu

F32 = jnp.float32
BF16 = jnp.bfloat16

EPS = 1e-6
CHUNK = 64
D_MODEL = 1024

SSD_D_INNER = 1024
SSD_HEAD_DIM = 64
SSD_N_HEADS = 16
SSD_N_GROUPS = 4
SSD_HEADS_PER_GROUP = 4
SSD_D_STATE = 128
SSD_CONV = 4
SSD_XBC = 2048
SSD_GROUP_W = SSD_HEADS_PER_GROUP * SSD_HEAD_DIM
SSD_HALO = 16

CONV_D = 1024
CONV_K = 31
CONV_HALO = 32

MLA_HEADS = 8
MLA_Q_RANK = 384
MLA_Q_RANK_PAD = 512
MLA_KV_RANK = 256
MLA_NOPE = 128
MLA_ROPE = 64
MLA_V = 128
MLA_QK = MLA_NOPE + MLA_ROPE
MLA_QK_PAD = 256
ROPE_THETA = 10000.0

X_HEADS = 4
X_HEAD_DIM = 256
FFN_HIDDEN = 2816

LANE = 128
SUBLANE = 8

OFF_GLU_A = 0
OFF_GLU_G = 1024
OFF_XBC = 2048
OFF_GATE = 4096
OFF_Z = 7168
OFF_QLAT = 8192
OFF_CKV = 8704
OFF_MISC = 8960
OFF_KROT = 9088
PROJ_W = 9216
MISC_DT = 64
TAIL_W = PROJ_W - OFF_MISC
TAIL_MISC = 0
TAIL_KROT = 1

VMEM_LIMIT = 56 * 1024 * 1024


def _cparams(sem):
    return pltpu.CompilerParams(dimension_semantics=sem, vmem_limit_bytes=VMEM_LIMIT)


def _dot(a, b):
    return jnp.dot(a, b, preferred_element_type=F32)


def _dot_nt(a, b):
    return lax.dot_general(a, b, (((1,), (1,)), ((), ())), preferred_element_type=F32)


def _split3(x):
    hi = x.astype(BF16)
    r = x - hi.astype(F32)
    mid = r.astype(BF16)
    lo = (r - mid.astype(F32)).astype(BF16)
    return hi, mid, lo


def _dot_exact_rhs01(x, m01):
    hi, mid, lo = _split3(x)
    return _dot(hi, m01) + _dot(mid, m01) + _dot(lo, m01)


def _dot_exact_lhs01(m01, x):
    hi, mid, lo = _split3(x)
    return _dot(m01, hi) + _dot(m01, mid) + _dot(m01, lo)


def _sigmoid(x):
    return 1.0 / (1.0 + jnp.exp(-x))


def _silu(x):
    return x * _sigmoid(x)


CONV_LB = CONV_D // LANE
CONV_LB_PER_STEP = 3
CONV_RB = 64
PROJ_CHUNK = TAIL_W


def _in_proj_conv_kernel(x_ref, g_ref, w_ref, cw_ref, cbias_ref, lng_ref, lnb_ref, o_ref, tail_ref, conv_ref,
                         u_ref, vpad_ref, sh_ref, cacc_ref, *, tm, blocks_per_seq):
    i = pl.program_id(0)
    j = pl.program_id(1)
    nj = pl.num_programs(1)
    n = CONV_HALO + tm
    first = CONV_HALO - (CONV_K - 1)

    def project():
        acc = _dot(u_ref[...], w_ref[...])
        o_ref[...] = acc.astype(o_ref.dtype)
        return acc

    @pl.when((j == 0) & (i % blocks_per_seq == 0))
    def _():
        vpad_ref[:, 0:CONV_HALO, :] = jnp.zeros((CONV_LB, CONV_HALO, LANE), F32)

    @pl.when((j == 0) & (i % blocks_per_seq != 0))
    def _():
        vpad_ref[:, 0:CONV_HALO, :] = vpad_ref[:, tm:n, :]

    @pl.when(j == 0)
    def _():
        x = x_ref[...]
        ms = jnp.mean(x * x, axis=-1, keepdims=True)
        u_ref[...] = (x * lax.rsqrt(ms + EPS) * g_ref[...]).astype(BF16)
        acc = project()
        v = acc[:, OFF_GLU_A:OFF_GLU_A + CONV_D] * _sigmoid(acc[:, OFF_GLU_G:OFF_GLU_G + CONV_D])
        for cb in range(CONV_LB):
            vpad_ref[cb, CONV_HALO:n, :] = v[:, cb * LANE:(cb + 1) * LANE]

    def shift_task(cb):
        def run():
            vp = vpad_ref[cb, 0:n, :]
            for r in range(1, SUBLANE):
                sh_ref[r - 1, 0:n, :] = pltpu.roll(vp, n - r, axis=0)
        return run

    def tap_task(cb, r0):
        def run():
            cacc = jnp.broadcast_to(cbias_ref[cb], (CONV_RB, LANE))
            for tap in range(CONV_K):
                q, r = divmod(first + tap, SUBLANE)
                start = r0 + q * SUBLANE
                win = (vpad_ref[cb, start:start + CONV_RB, :] if r == 0
                       else sh_ref[r - 1, start:start + CONV_RB, :])
                cacc = cacc + cw_ref[cb, tap:tap + 1, :] * win
            cacc_ref[cb, r0:r0 + CONV_RB, :] = cacc
        return run

    @pl.when(j > 0)
    def _():
        tasks = []
        for c in range(CONV_LB_PER_STEP):
            cb = jnp.minimum((j - 1) * CONV_LB_PER_STEP + c, CONV_LB - 1)
            tasks.append(shift_task(cb))
            tasks.extend(tap_task(cb, r0) for r0 in range(0, tm, CONV_RB))
        tn = o_ref.shape[1]
        n_chunks = tn // PROJ_CHUNK
        per_chunk = -(-len(tasks) // n_chunks)
        u = u_ref[...]
        for k in range(n_chunks):
            cols = slice(k * PROJ_CHUNK, (k + 1) * PROJ_CHUNK)
            acc = _dot(u, w_ref[:, cols])
            o_ref[:, cols] = acc.astype(o_ref.dtype)
            for task in tasks[k * per_chunk:(k + 1) * per_chunk]:
                task()

        @pl.when(j == nj - 1)
        def _():
            tail_ref[...] = acc

    @pl.when(j == nj - 1)
    def _():
        v = jnp.concatenate([cacc_ref[cb] for cb in range(CONV_LB)], axis=-1)
        mu = jnp.mean(v, axis=-1, keepdims=True)
        vc = v - mu
        var = jnp.mean(vc * vc, axis=-1, keepdims=True)
        y = vc * lax.rsqrt(var + EPS) * lng_ref[...] + lnb_ref[...]
        conv_ref[...] = _silu(y).astype(conv_ref.dtype)


def _in_proj_conv(x2, g, w, cw, cbias, lng, lnb, tm, tn, blocks_per_seq):
    t, d = x2.shape
    n = w.shape[1]
    assert (n // tn - 1) * CONV_LB_PER_STEP >= CONV_LB and OFF_GLU_G + CONV_D <= tn
    full = lambda shape: pl.BlockSpec(shape, lambda i, j: (0,) * len(shape))
    rows = CONV_HALO + tm + SUBLANE
    return pl.pallas_call(
        functools.partial(_in_proj_conv_kernel, tm=tm, blocks_per_seq=blocks_per_seq),
        grid=(t // tm, n // tn),
        in_specs=[
            pl.BlockSpec((tm, d), lambda i, j: (i, 0)),
            full((1, d)),
            pl.BlockSpec((d, tn), lambda i, j: (0, j)),
            full((CONV_LB, CONV_K, LANE)), full((CONV_LB, 1, LANE)), full((1, CONV_D)), full((1, CONV_D)),
        ],
        out_specs=[pl.BlockSpec((tm, tn), lambda i, j: (i, j)), pl.BlockSpec((tm, TAIL_W), lambda i, j: (i, 0)),
                   pl.BlockSpec((tm, CONV_D), lambda i, j: (i, 0))],
        out_shape=[jax.ShapeDtypeStruct((t, n), BF16), jax.ShapeDtypeStruct((t, TAIL_W), F32),
                   jax.ShapeDtypeStruct((t, CONV_D), BF16)],
        scratch_shapes=[pltpu.VMEM((tm, d), BF16),
                        pltpu.VMEM((CONV_LB, rows, LANE), F32),
                        pltpu.VMEM((SUBLANE - 1, rows, LANE), F32),
                        pltpu.VMEM((CONV_LB, tm, LANE), F32)],
        compiler_params=_cparams(("arbitrary", "arbitrary")),
        name="in_proj_conv",
    )(x2, g, w, cw, cbias, lng, lnb)


def _ssd_kernel(xbc_ref, z_ref, misc_ref, convw_ref, convb_ref, dtb_ref, alog_s_ref,
                dskip_ref, normg_ref, expand_ref, tri_ref, shift_ref, y_ref, xprev_ref, state_ref, *, L, n_chunks):
    ts = L * n_chunks

    @pl.when(pl.program_id(1) == 0)
    def _():
        xprev_ref[...] = jnp.zeros(xprev_ref.shape, xprev_ref.dtype)
        state_ref[...] = jnp.zeros(state_ref.shape, F32)

    xb = xbc_ref[0]
    xe = jnp.concatenate([xprev_ref[...], xb], axis=0)
    xprev_ref[...] = xb[ts - SSD_HALO:ts]
    parts = []
    for c in range(n_chunks):
        sh = _dot(shift_ref[...], xe[c * L:c * L + L + SSD_HALO])
        part = convb_ref[...] + convw_ref[SSD_CONV - 1:SSD_CONV, :] * xb[c * L:(c + 1) * L].astype(F32)
        for d in range(1, SSD_CONV):
            part = part + convw_ref[SSD_CONV - 1 - d:SSD_CONV - d, :] * sh[(d - 1) * L:d * L]
        parts.append(part)
    xc = _silu(jnp.concatenate(parts, axis=0))

    misc = misc_ref[0]
    v = misc + dtb_ref[...]
    dt_s = jnp.maximum(v, 0.0) + jnp.log1p(jnp.exp(-jnp.abs(v)))
    a_neg_s = -jnp.exp(alog_s_ref[...])
    expand = expand_ref[...]
    tri = tri_ref[...]
    dt_e = _dot_exact_rhs01(dt_s, expand)
    a_s = dt_s * a_neg_s

    row_i = lax.broadcasted_iota(jnp.int32, (L, L), 0)
    col_i = lax.broadcasted_iota(jnp.int32, (L, L), 1)
    causal = row_i >= col_i
    lane_g = lax.broadcasted_iota(jnp.int32, (1, SSD_GROUP_W), 1) // SSD_HEAD_DIM

    z = z_ref[0].astype(F32)
    for c in range(n_chunks):
        r0 = c * L
        xs = xc[r0:r0 + L, 0:SSD_D_INNER]
        acs_s = _dot_exact_lhs01(tri, a_s[r0:r0 + L])
        acs_e = _dot_exact_rhs01(acs_s, expand)
        acs_st = acs_s.T
        last_e = acs_e[L - 1:L, :]
        exp_acs = jnp.exp(acs_e)
        to_end = jnp.exp(last_e - acs_e)
        chunk_decay = jnp.exp(last_e)
        xdt = xs * dt_e[r0:r0 + L]
        xdt_b = xdt.astype(BF16)
        xend_b = (xdt * to_end).astype(BF16)
        y_groups = []
        for g in range(SSD_N_GROUPS):
            gs = slice(g * SSD_GROUP_W, (g + 1) * SSD_GROUP_W)
            b_g = xc[r0:r0 + L, SSD_D_INNER + g * SSD_D_STATE:SSD_D_INNER + (g + 1) * SSD_D_STATE]
            c_g = xc[r0:r0 + L, SSD_D_INNER + (SSD_N_GROUPS + g) * SSD_D_STATE:
                     SSD_D_INNER + (SSD_N_GROUPS + g + 1) * SSD_D_STATE]
            b_gt = b_g.T.astype(BF16)
            c_gb = c_g.astype(BF16)
            cb = _dot(c_gb, b_gt)
            st = state_ref[g]
            y_g = _dot(c_gb, st.astype(BF16)) * exp_acs[:, gs]
            x_g = xdt_b[:, gs]
            for r in range(SSD_HEADS_PER_GROUP):
                h = g * SSD_HEADS_PER_GROUP + r
                col = acs_s[:, MISC_DT + h:MISC_DT + h + 1]
                row = acs_st[MISC_DT + h:MISC_DT + h + 1, :]
                decay = jnp.exp(jnp.where(causal, col - row, -jnp.inf))
                gm = (cb * decay).astype(BF16)
                x_h = jnp.where(lane_g == r, x_g, jnp.zeros_like(x_g))
                y_g = y_g + _dot(gm, x_h)
            state_ref[g] = st * chunk_decay[:, gs] + _dot(b_gt, xend_b[:, gs])
            y_groups.append(y_g)
        y = jnp.concatenate(y_groups, axis=-1) + dskip_ref[...] * xs
        y = y * _silu(z[r0:r0 + L])
        outs = []
        for g in range(SSD_N_GROUPS):
            yg = y[:, g * SSD_GROUP_W:(g + 1) * SSD_GROUP_W]
            outs.append(yg * lax.rsqrt(jnp.mean(yg * yg, axis=-1, keepdims=True) + EPS))
        y = jnp.concatenate(outs, axis=-1) * normg_ref[...]
        y_ref[0, r0:r0 + L, :] = y.astype(y_ref.dtype)


def _ssd(proj3, tail3, convw, convb, dtb, alog_s, dskip_e, normg, expand, tri, shift, L, n_chunks):
    b, s, _ = proj3.shape
    ts = L * n_chunks
    full = lambda shape: pl.BlockSpec(shape, lambda bi, si: (0,) * len(shape))
    return pl.pallas_call(
        functools.partial(_ssd_kernel, L=L, n_chunks=n_chunks),
        grid=(b, s // ts),
        in_specs=[
            pl.BlockSpec((1, ts, SSD_XBC), lambda bi, si: (bi, si, OFF_XBC // SSD_XBC)),
            pl.BlockSpec((1, ts, SSD_D_INNER), lambda bi, si: (bi, si, OFF_Z // SSD_D_INNER)),
            pl.BlockSpec((1, ts, LANE), lambda bi, si: (bi, si, TAIL_MISC)),
            full((SSD_CONV, SSD_XBC)), full((1, SSD_XBC)), full((1, LANE)), full((1, LANE)),
            full((1, SSD_D_INNER)), full((1, SSD_D_INNER)),
            full((LANE, SSD_D_INNER)), full((L, L)), full(shift.shape),
        ],
        out_specs=pl.BlockSpec((1, ts, SSD_D_INNER), lambda bi, si: (bi, si, 0)),
        out_shape=jax.ShapeDtypeStruct((b, s, SSD_D_INNER), BF16),
        scratch_shapes=[
            pltpu.VMEM((SSD_HALO, SSD_XBC), BF16),
            pltpu.VMEM((SSD_N_GROUPS, SSD_D_STATE, SSD_GROUP_W), F32),
        ],
        compiler_params=_cparams(("parallel", "arbitrary")),
        name="ssd",
    )(proj3, proj3, tail3, convw, convb, dtb, alog_s, dskip_e, normg, expand, tri, shift)


def _mla_prep_kernel(qlat_ref, ckv_ref, misc_ref, krot_ref, cos_ref, sin_ref, qag_ref, wq_ref, kvag_ref,
                     wkv_ref, qg_nope_ref, qg_rope_ref, qg_rot_ref, kg_nope_ref, kg_rope_ref, kg_rot_ref,
                     q_ref, k_ref, v_ref):
    hn = MLA_HEADS * MLA_NOPE
    cos2 = cos_ref[0]
    sin2 = sin_ref[0]
    scale = MLA_QK ** -0.5 * math.log2(math.e)

    ql = qlat_ref[0].astype(F32)
    ms = jnp.sum(ql * ql, axis=-1, keepdims=True) * (1.0 / MLA_Q_RANK)
    qa = (ql * lax.rsqrt(ms + EPS) * qag_ref[...]).astype(BF16)
    qf = _dot(qa, wq_ref[...])
    for h in range(MLA_HEADS):
        qn = qf[:, h * LANE:(h + 1) * LANE]
        qn = qn * lax.rsqrt(jnp.mean(qn * qn, axis=-1, keepdims=True) + EPS) * qg_nope_ref[...]
        qr = qf[:, hn + h * LANE:hn + (h + 1) * LANE]
        qs = qf[:, 2 * hn + h * LANE:2 * hn + (h + 1) * LANE]
        inv = lax.rsqrt(jnp.sum(qr * qr, axis=-1, keepdims=True) * (1.0 / MLA_ROPE) + EPS)
        qro = (qr * qg_rope_ref[...] * cos2 + qs * qg_rot_ref[...] * sin2) * inv
        q_ref[0, :, h * MLA_QK_PAD:h * MLA_QK_PAD + LANE] = (qn * scale).astype(q_ref.dtype)
        q_ref[0, :, h * MLA_QK_PAD + LANE:(h + 1) * MLA_QK_PAD] = (qro * scale).astype(q_ref.dtype)

    ckv = ckv_ref[0].astype(F32)
    ca = (ckv * lax.rsqrt(jnp.mean(ckv * ckv, axis=-1, keepdims=True) + EPS) * kvag_ref[...]).astype(BF16)
    kvf = _dot(ca, wkv_ref[...])
    lane = lax.broadcasted_iota(jnp.int32, (1, LANE), 1)
    kr = jnp.where(lane < MLA_ROPE, misc_ref[0], 0.0)
    ks = krot_ref[0]
    inv = lax.rsqrt(jnp.sum(kr * kr, axis=-1, keepdims=True) * (1.0 / MLA_ROPE) + EPS)
    kro_t = ((kr * kg_rope_ref[...] * cos2 + ks * kg_rot_ref[...] * sin2) * inv).T.astype(k_ref.dtype)
    for h in range(MLA_HEADS):
        kn = kvf[:, h * LANE:(h + 1) * LANE]
        kn = kn * lax.rsqrt(jnp.mean(kn * kn, axis=-1, keepdims=True) + EPS) * kg_nope_ref[...]
        k_ref[0, h, 0, 0:LANE, :] = kn.T.astype(k_ref.dtype)
        k_ref[0, h, 0, LANE:MLA_QK_PAD, :] = kro_t
    v_ref[0] = kvf[:, hn:].astype(v_ref.dtype)


def _mla_prep(proj3, tail3, cos2, sin2, qag, wq, kvag, wkv, qg_nope, qg_rope, qg_rot, kg_nope, kg_rope, kg_rot, tm):
    b, s, _ = proj3.shape
    full = lambda shape: pl.BlockSpec(shape, lambda bi, si: (0,) * len(shape))
    hq = MLA_HEADS * MLA_QK_PAD
    hv = MLA_HEADS * MLA_V
    return pl.pallas_call(
        _mla_prep_kernel,
        grid=(b, s // tm),
        in_specs=[
            pl.BlockSpec((1, tm, MLA_Q_RANK_PAD), lambda bi, si: (bi, si, OFF_QLAT // MLA_Q_RANK_PAD)),
            pl.BlockSpec((1, tm, MLA_KV_RANK), lambda bi, si: (bi, si, OFF_CKV // MLA_KV_RANK)),
            pl.BlockSpec((1, tm, LANE), lambda bi, si: (bi, si, TAIL_MISC)),
            pl.BlockSpec((1, tm, LANE), lambda bi, si: (bi, si, TAIL_KROT)),
            pl.BlockSpec((1, tm, LANE), lambda bi, si: (bi, si, 0)),
            pl.BlockSpec((1, tm, LANE), lambda bi, si: (bi, si, 0)),
            full((1, MLA_Q_RANK_PAD)), full(wq.shape), full((1, MLA_KV_RANK)), full(wkv.shape),
            full((1, LANE)), full((1, LANE)), full((1, LANE)), full((1, LANE)), full((1, LANE)), full((1, LANE)),
        ],
        out_specs=[
            pl.BlockSpec((1, tm, hq), lambda bi, si: (bi, si, 0)),
            pl.BlockSpec((1, MLA_HEADS, 1, MLA_QK_PAD, tm), lambda bi, si: (bi, 0, si, 0, 0)),
            pl.BlockSpec((1, tm, hv), lambda bi, si: (bi, si, 0)),
        ],
        out_shape=[
            jax.ShapeDtypeStruct((b, s, hq), BF16),
            jax.ShapeDtypeStruct((b, MLA_HEADS, s // tm, MLA_QK_PAD, tm), BF16),
            jax.ShapeDtypeStruct((b, s, hv), BF16),
        ],
        compiler_params=_cparams(("parallel", "parallel")),
        name="mla_prep",
    )(proj3, proj3, tail3, tail3, cos2, sin2, qag, wq, kvag, wkv, qg_nope, qg_rope, qg_rot, kg_nope, kg_rope,
      kg_rot)


def _attn_kernel(q_ref, kt_ref, v_ref, o_ref, sa_ref, sb_ref, m_ref, acc_ref, *, tq):
    i = pl.program_id(2)
    q = q_ref[0]
    reps = tq // LANE

    def scores(j, s_ref):
        s_ref[...] = _dot(q, kt_ref[0, 0, j])

    def consume(j, s_ref, masked):
        start = pl.multiple_of(j * tq, tq)
        v = v_ref[0, pl.ds(start, tq), :]
        v1 = jnp.concatenate([v, jnp.ones_like(v)], axis=-1)
        s = s_ref[...]
        if masked:
            rq = lax.broadcasted_iota(jnp.int32, (tq, tq), 0) // CHUNK
            ck = lax.broadcasted_iota(jnp.int32, (tq, tq), 1) // CHUNK
            s = jnp.where(ck <= rq, s, -jnp.inf)
        m = m_ref[...]
        m_new = jnp.maximum(m, jnp.max(s, axis=-1, keepdims=True))
        p = jnp.exp2(s - jnp.concatenate([m_new] * reps, axis=-1))
        alpha = jnp.exp2(m - m_new)
        acc_ref[...] = jnp.concatenate([alpha, alpha], axis=-1) * acc_ref[...] + _dot(p.astype(BF16), v1)
        m_ref[...] = m_new

    m_ref[...] = jnp.full(m_ref.shape, -jnp.inf, F32)
    acc_ref[...] = jnp.zeros(acc_ref.shape, F32)
    scores(0, sa_ref)

    def pair(jj, carry):
        j = 2 * jj
        scores(j + 1, sb_ref)
        consume(j, sa_ref, False)
        scores(j + 2, sa_ref)
        consume(j + 1, sb_ref, False)
        return carry

    lax.fori_loop(0, i // 2, pair, 0)

    @pl.when(i % 2 == 0)
    def _():
        consume(i, sa_ref, True)

    @pl.when(i % 2 == 1)
    def _():
        scores(i, sb_ref)
        consume(i - 1, sa_ref, False)
        consume(i, sb_ref, True)

    acc = acc_ref[...]
    o_ref[0] = (acc[:, :MLA_V] / acc[:, MLA_V:]).astype(o_ref.dtype)


def _attention(q, kt, v, tq):
    b, s, _ = q.shape
    n = s // tq
    return pl.pallas_call(
        functools.partial(_attn_kernel, tq=tq),
        grid=(b, MLA_HEADS, n),
        in_specs=[
            pl.BlockSpec((1, tq, MLA_QK_PAD), lambda bi, h, i: (bi, i, h)),
            pl.BlockSpec((1, 1, n, MLA_QK_PAD, tq), lambda bi, h, i: (bi, h, 0, 0, 0)),
            pl.BlockSpec((1, s, MLA_V), lambda bi, h, i: (bi, 0, h)),
        ],
        out_specs=pl.BlockSpec((1, tq, MLA_V), lambda bi, h, i: (bi, i, h)),
        out_shape=jax.ShapeDtypeStruct((b, s, MLA_HEADS * MLA_V), BF16),
        scratch_shapes=[pltpu.VMEM((tq, tq), F32), pltpu.VMEM((tq, tq), F32), pltpu.VMEM((tq, LANE), F32),
                        pltpu.VMEM((tq, 2 * MLA_V), F32)],
        compiler_params=_cparams(("parallel", "parallel", "arbitrary")),
        name="mla_attention",
    )(q, kt, v)


def _merge_kernel(x_ref, ssd_ref, conv_ref, att_ref, g0_ref, g1_ref, g2_ref, gb_ref, w_ssd_ref, w_conv_ref,
                  w_mla_ref, w_out_ref, o_ref):
    merged = _sigmoid(g0_ref[...].astype(F32) + gb_ref[0:1, :]) * _dot(ssd_ref[...], w_ssd_ref[...])
    merged = merged + _sigmoid(g1_ref[...].astype(F32) + gb_ref[1:2, :]) * _dot(conv_ref[...], w_conv_ref[...])
    merged = merged + _sigmoid(g2_ref[...].astype(F32) + gb_ref[2:3, :]) * _dot(att_ref[...], w_mla_ref[...])
    o_ref[...] = x_ref[...] + _dot(merged.astype(BF16), w_out_ref[...])


def _merge(x2, ssd_pre, conv_pre, att, proj2, gate_b, w_ssd, w_conv, w_mla, w_out, tm):
    t, d = x2.shape
    row = lambda c: pl.BlockSpec((tm, d), lambda i, c=c: (i, c))
    full = lambda shape: pl.BlockSpec(shape, lambda i: (0,) * len(shape))
    g0 = OFF_GATE // d
    return pl.pallas_call(
        _merge_kernel,
        grid=(t // tm,),
        in_specs=[row(0), row(0), row(0), row(0), row(g0), row(g0 + 1), row(g0 + 2), full(gate_b.shape),
                  full((d, d)), full((d, d)), full((d, d)), full((d, d))],
        out_specs=row(0),
        out_shape=jax.ShapeDtypeStruct((t, d), F32),
        compiler_params=_cparams(("parallel",)),
        name="merge",
    )(x2, ssd_pre, conv_pre, att, proj2, proj2, proj2, gate_b, w_ssd, w_conv, w_mla, w_out)


def _mem_kv_kernel(mem_ref, g_ref, w_ref, kg_ref, k_ref, v_ref):
    m = mem_ref[0]
    mn = (m * lax.rsqrt(jnp.mean(m * m, axis=-1, keepdims=True) + EPS) * g_ref[...]).astype(BF16)
    kv = _dot(mn, w_ref[...])
    for h in range(X_HEADS):
        kh = kv[:, h * X_HEAD_DIM:(h + 1) * X_HEAD_DIM]
        kh = kh * lax.rsqrt(jnp.mean(kh * kh, axis=-1, keepdims=True) + EPS) * kg_ref[...]
        k_ref[0, :, h * X_HEAD_DIM:(h + 1) * X_HEAD_DIM] = kh.astype(k_ref.dtype)
    v_ref[0] = kv[:, D_MODEL:].astype(v_ref.dtype)


def _mem_kv(mem, g, w_kv, kg):
    b, m, d = mem.shape
    full = lambda shape: pl.BlockSpec(shape, lambda bi: (0,) * len(shape))
    blk = pl.BlockSpec((1, m, d), lambda bi: (bi, 0, 0))
    return pl.pallas_call(
        _mem_kv_kernel,
        grid=(b,),
        in_specs=[blk, full((1, d)), full(w_kv.shape), full((1, X_HEAD_DIM))],
        out_specs=[blk, blk],
        out_shape=[jax.ShapeDtypeStruct((b, m, d), BF16), jax.ShapeDtypeStruct((b, m, d), BF16)],
        compiler_params=_cparams(("parallel",)),
        name="mem_kv",
    )(mem, g, w_kv, kg)


def _xattn_kernel(x_ref, k_ref, v_ref, g_ref, wq_ref, qg_ref, wo_ref, o_ref):
    x = x_ref[0]
    u = (x * lax.rsqrt(jnp.mean(x * x, axis=-1, keepdims=True) + EPS) * g_ref[...]).astype(BF16)
    q = _dot(u, wq_ref[...])
    scale = X_HEAD_DIM ** -0.5
    outs = []
    for h in range(X_HEADS):
        hs = slice(h * X_HEAD_DIM, (h + 1) * X_HEAD_DIM)
        qh = q[:, hs]
        qh = qh * lax.rsqrt(jnp.mean(qh * qh, axis=-1, keepdims=True) + EPS) * (qg_ref[...] * scale)
        s = _dot_nt(qh.astype(BF16), k_ref[0, :, hs])
        p = jnp.exp(s - jnp.max(s, axis=-1, keepdims=True))
        o = _dot(p.astype(BF16), v_ref[0, :, hs])
        outs.append((o / jnp.sum(p, axis=-1, keepdims=True)).astype(BF16))
    o_ref[0] = x + _dot(jnp.concatenate(outs, axis=-1), wo_ref[...])


def _xattn(x3, k, v, g, wq, qg, wo, tm):
    b, s, d = x3.shape
    m = k.shape[1]
    full = lambda shape: pl.BlockSpec(shape, lambda bi, si: (0,) * len(shape))
    return pl.pallas_call(
        _xattn_kernel,
        grid=(b, s // tm),
        in_specs=[
            pl.BlockSpec((1, tm, d), lambda bi, si: (bi, si, 0)),
            pl.BlockSpec((1, m, d), lambda bi, si: (bi, 0, 0)),
            pl.BlockSpec((1, m, d), lambda bi, si: (bi, 0, 0)),
            full((1, d)), full((d, d)), full((1, X_HEAD_DIM)), full((d, d)),
        ],
        out_specs=pl.BlockSpec((1, tm, d), lambda bi, si: (bi, si, 0)),
        out_shape=jax.ShapeDtypeStruct((b, s, d), F32),
        compiler_params=_cparams(("parallel", "parallel")),
        name="mem_xattn",
    )(x3, k, v, g, wq, qg, wo)


def _ffn_kernel(x_ref, g_ref, wg_ref, wu_ref, wo_ref, o_ref, *, hc):
    x = x_ref[...]
    u = (x * lax.rsqrt(jnp.mean(x * x, axis=-1, keepdims=True) + EPS) * g_ref[...]).astype(BF16)
    acc = x
    for c in range(0, FFN_HIDDEN, hc):
        gate = _dot(u, wg_ref[:, c:c + hc])
        up = _dot(u, wu_ref[:, c:c + hc])
        acc = acc + _dot((_silu(gate) * up).astype(BF16), wo_ref[c:c + hc, :])
    o_ref[...] = acc


def _ffn(x2, g, wg, wu, wo, tm, hc):
    t, d = x2.shape
    full = lambda shape: pl.BlockSpec(shape, lambda i: (0,) * len(shape))
    return pl.pallas_call(
        functools.partial(_ffn_kernel, hc=hc),
        grid=(t // tm,),
        in_specs=[pl.BlockSpec((tm, d), lambda i: (i, 0)), full((1, d)), full(wg.shape), full(wu.shape),
                  full(wo.shape)],
        out_specs=pl.BlockSpec((tm, d), lambda i: (i, 0)),
        out_shape=jax.ShapeDtypeStruct((t, d), F32),
        compiler_params=_cparams(("parallel",)),
        name="ffn",
    )(x2, g, wg, wu, wo)


def _swap_halves(a):
    h = a.shape[-1] // 2
    return jnp.concatenate([a[..., h:], a[..., :h]], axis=-1)


def _pad_last(a, width):
    return jnp.pad(a, [(0, 0)] * (a.ndim - 1) + [(0, width - a.shape[-1])])


def _in_proj_weight(w):
    z, xbc, dt, glu, q_lat, kv_lat, gates = jnp.split(
        w, np.cumsum([SSD_D_INNER, SSD_XBC, SSD_N_HEADS, 2 * CONV_D, MLA_Q_RANK, MLA_KV_RANK + MLA_ROPE])
        .tolist(), axis=-1)
    c_kv, k_rope = kv_lat[:, :MLA_KV_RANK], kv_lat[:, MLA_KV_RANK:]
    misc = _pad_last(jnp.concatenate([k_rope, dt], axis=-1), LANE)
    krot = _pad_last(_swap_halves(k_rope), LANE)
    out = jnp.concatenate([glu, xbc, gates, z, _pad_last(q_lat, MLA_Q_RANK_PAD), c_kv, misc, krot], axis=-1)
    return _pad_last(out, PROJ_W).astype(BF16)


def _mla_q_weight(w):
    w = w.reshape(MLA_Q_RANK, MLA_HEADS, MLA_QK)
    nope = w[:, :, :MLA_NOPE].reshape(MLA_Q_RANK, -1)
    rope = w[:, :, MLA_NOPE:]
    rope_p = _pad_last(rope, LANE).reshape(MLA_Q_RANK, -1)
    rot_p = _pad_last(_swap_halves(rope), LANE).reshape(MLA_Q_RANK, -1)
    out = jnp.concatenate([nope, rope_p, rot_p], axis=-1)
    return jnp.pad(out, [(0, MLA_Q_RANK_PAD - MLA_Q_RANK), (0, 0)]).astype(BF16)


def _mla_kv_weight(w):
    w = w.reshape(MLA_KV_RANK, MLA_HEADS, MLA_NOPE + MLA_V)
    return jnp.concatenate([w[:, :, :MLA_NOPE].reshape(MLA_KV_RANK, -1),
                            w[:, :, MLA_NOPE:].reshape(MLA_KV_RANK, -1)], axis=-1).astype(BF16)


def _row(a, width=None):
    a = a.reshape(1, -1).astype(F32)
    return a if width is None else _pad_last(a, width)


def _pick(n, prefs):
    for p in prefs:
        if n % p == 0:
            return p
    return n


def kernel(x, mem, positions, mix_norm_g, w_in, ssd_conv_w, ssd_conv_b, ssd_dt_bias, ssd_a_log, ssd_d, ssd_norm_g, ssd_w_out, conv_dw_w, conv_dw_b, conv_ln_g, conv_ln_b, conv_w_out, mla_q_a_g, mla_w_q_b, mla_kv_a_g, mla_w_kv_b, mla_q_norm_g, mla_k_norm_g, mla_w_o, gate_b, w_out, xattn_norm_g, mem_norm_g, xattn_w_q, xattn_w_kv, xattn_q_norm_g, xattn_k_norm_g, xattn_w_o, ffn_norm_g, ffn_w_in, ffn_w_out):
    b, s, d = x.shape
    t = b * s
    depth = w_in.shape[0]

    inv = ROPE_THETA ** (-jnp.arange(0, MLA_ROPE, 2, dtype=F32) / MLA_ROPE)
    ang = positions.astype(F32)[..., None] * inv
    cos, sin = jnp.cos(ang), jnp.sin(ang)
    cos2 = _pad_last(jnp.concatenate([cos, cos], axis=-1), LANE)
    sin2 = _pad_last(jnp.concatenate([-sin, sin], axis=-1), LANE)

    ssd_l = _pick(s, (128, 64))
    ssd_nc = _pick(s // ssd_l, (4, 2, 1))
    head_of_lane = np.arange(SSD_D_INNER) // SSD_HEAD_DIM
    expand = jnp.asarray((np.arange(LANE)[:, None] - MISC_DT) == head_of_lane[None, :], dtype=BF16)
    tri = jnp.asarray(np.tril(np.ones((ssd_l, ssd_l), np.float32)), dtype=BF16)
    shift_np = np.zeros((SSD_CONV - 1, ssd_l, ssd_l + SSD_HALO), np.float32)
    for dd in range(1, SSD_CONV):
        shift_np[dd - 1, np.arange(ssd_l), SSD_HALO + np.arange(ssd_l) - dd] = 1.0
    shift = jnp.asarray(shift_np.reshape((SSD_CONV - 1) * ssd_l, ssd_l + SSD_HALO), dtype=BF16)

    tm_proj = _pick(s, (1024, 512, 256))
    tm_tok = _pick(t, (512, 256))
    tq = _pick(s, (512, 256, 128))

    x2 = x.reshape(t, d)
    for l in range(depth):
        conv_w_lb = conv_dw_w[l].astype(F32).reshape(CONV_K, CONV_LB, LANE).transpose(1, 0, 2)
        proj, tail, conv_pre = _in_proj_conv(
            x2, _row(mix_norm_g[l]), _in_proj_weight(w_in[l]), conv_w_lb,
            conv_dw_b[l].astype(F32).reshape(CONV_LB, 1, LANE), _row(conv_ln_g[l]), _row(conv_ln_b[l]),
            tm_proj, 2304, s // tm_proj)
        proj3 = proj.reshape(b, s, PROJ_W)
        tail3 = tail.reshape(b, s, TAIL_W)

        dtb = jnp.pad(_row(ssd_dt_bias[l]), [(0, 0), (MISC_DT, LANE - MISC_DT - SSD_N_HEADS)])
        alog_s = jnp.pad(_row(ssd_a_log[l]), [(0, 0), (MISC_DT, LANE - MISC_DT - SSD_N_HEADS)])
        dskip_e = jnp.repeat(_row(ssd_d[l]), SSD_HEAD_DIM, axis=-1)
        ssd_pre = _ssd(proj3, tail3, ssd_conv_w[l], _row(ssd_conv_b[l]), dtb, alog_s, dskip_e,
                       _row(ssd_norm_g[l]), expand, tri, shift, ssd_l, ssd_nc)

        qg, kg = mla_q_norm_g[l], mla_k_norm_g[l]
        q, k, v = _mla_prep(
            proj3, tail3, cos2, sin2, _row(mla_q_a_g[l], MLA_Q_RANK_PAD), _mla_q_weight(mla_w_q_b[l]),
            _row(mla_kv_a_g[l]), _mla_kv_weight(mla_w_kv_b[l]),
            _row(qg[:MLA_NOPE]), _row(qg[MLA_NOPE:], LANE), _row(_swap_halves(qg[MLA_NOPE:]), LANE),
            _row(kg[:MLA_NOPE]), _row(kg[MLA_NOPE:], LANE), _row(_swap_halves(kg[MLA_NOPE:]), LANE), tq)
        att = _attention(q, k, v, tq)

        x2 = _merge(x2, ssd_pre.reshape(t, d), conv_pre.reshape(t, d), att.reshape(t, d), proj,
                    gate_b[l].astype(F32), ssd_w_out[l].astype(BF16), conv_w_out[l].astype(BF16),
                    mla_w_o[l].astype(BF16), w_out[l].astype(BF16), tm_tok)

        mk, mv = _mem_kv(mem, _row(mem_norm_g[l]), xattn_w_kv[l].astype(BF16), _row(xattn_k_norm_g[l]))
        x2 = _xattn(x2.reshape(b, s, d), mk, mv, _row(xattn_norm_g[l]), xattn_w_q[l].astype(BF16),
                    _row(xattn_q_norm_g[l]), xattn_w_o[l].astype(BF16), tm_tok).reshape(t, d)

        w_ffn = ffn_w_in[l].astype(BF16)
        x2 = _ffn(x2, _row(ffn_norm_g[l]), w_ffn[:, :FFN_HIDDEN], w_ffn[:, FFN_HIDDEN:],
                  ffn_w_out[l].astype(BF16), tm_tok, 256)
    return x2.reshape(b, s, d)
```

```python
import functools
import math

import jax
import jax.numpy as jnp
import numpy as np
from jax import lax
from jax.experimental import pallas as pl
from jax.experimental.pallas import tpu as pltpu

F32 = jnp.float32
BF16 = jnp.bfloat16

EPS = 1e-6
CHUNK = 64
D_MODEL = 1024

SSD_D_INNER = 1024
SSD_HEAD_DIM = 64
SSD_N_HEADS = 16
SSD_N_GROUPS = 4
SSD_HEADS_PER_GROUP = 4
SSD_D_STATE = 128
SSD_CONV = 4
SSD_XBC = 2048
SSD_GROUP_W = SSD_HEADS_PER_GROUP * SSD_HEAD_DIM
SSD_HALO = 16

CONV_D = 1024
CONV_K = 31
CONV_HALO = 32

MLA_HEADS = 8
MLA_Q_RANK = 384
MLA_Q_RANK_PAD = 512
MLA_KV_RANK = 256
MLA_NOPE = 128
MLA_ROPE = 64
MLA_V = 128
MLA_QK = MLA_NOPE + MLA_ROPE
MLA_QK_PAD = 256
ROPE_THETA = 10000.0
ATTN_ROW_SPLIT = 2

X_HEADS = 4
X_HEAD_DIM = 256
FFN_HIDDEN = 2816

LANE = 128
SUBLANE = 8

OFF_GLU_A = 0
OFF_GLU_G = 1024
OFF_XBC = 2048
OFF_GATE = 4096
OFF_Z = 7168
OFF_QLAT = 8192
OFF_CKV = 8704
OFF_MISC = 8960
OFF_KROT = 9088
PROJ_W = 9216
MISC_DT = 64
TAIL_W = PROJ_W - OFF_MISC
TAIL_MISC = 0
TAIL_KROT = 1

VMEM_LIMIT = 56 * 1024 * 1024


def _cparams(sem):
    return pltpu.CompilerParams(dimension_semantics=sem, vmem_limit_bytes=VMEM_LIMIT)


def _dot(a, b):
    return jnp.dot(a, b, preferred_element_type=F32)


def _dot_nt(a, b):
    return lax.dot_general(a, b, (((1,), (1,)), ((), ())), preferred_element_type=F32)


def _split3(x):
    hi = x.astype(BF16)
    r = x - hi.astype(F32)
    mid = r.astype(BF16)
    lo = (r - mid.astype(F32)).astype(BF16)
    return hi, mid, lo


def _dot_exact_rhs01(x, m01):
    hi, mid, lo = _split3(x)
    return _dot(hi, m01) + _dot(mid, m01) + _dot(lo, m01)


def _dot_exact_lhs01(m01, x):
    hi, mid, lo = _split3(x)
    return _dot(m01, hi) + _dot(m01, mid) + _dot(m01, lo)


def _sigmoid(x):
    return 1.0 / (1.0 + jnp.exp(-x))


def _silu(x):
    return x * _sigmoid(x)


CONV_LB = CONV_D // LANE
CONV_LB_PER_STEP = 3
CONV_RB = 64
PROJ_CHUNK = TAIL_W


def _in_proj_conv_kernel(x_ref, g_ref, w_ref, cw_ref, cbias_ref, lng_ref, lnb_ref, o_ref, tail_ref, conv_ref,
                         u_ref, vpad_ref, sh_ref, cacc_ref, *, tm, blocks_per_seq):
    i = pl.program_id(0)
    j = pl.program_id(1)
    nj = pl.num_programs(1)
    n = CONV_HALO + tm
    first = CONV_HALO - (CONV_K - 1)

    def project():
        acc = _dot(u_ref[...], w_ref[...])
        o_ref[...] = acc.astype(o_ref.dtype)
        return acc

    @pl.when((j == 0) & (i % blocks_per_seq == 0))
    def _():
        vpad_ref[:, 0:CONV_HALO, :] = jnp.zeros((CONV_LB, CONV_HALO, LANE), F32)

    @pl.when((j == 0) & (i % blocks_per_seq != 0))
    def _():
        vpad_ref[:, 0:CONV_HALO, :] = vpad_ref[:, tm:n, :]

    @pl.when(j == 0)
    def _():
        x = x_ref[...]
        ms = jnp.mean(x * x, axis=-1, keepdims=True)
        u_ref[...] = (x * lax.rsqrt(ms + EPS) * g_ref[...]).astype(BF16)
        acc = project()
        v = acc[:, OFF_GLU_A:OFF_GLU_A + CONV_D] * _sigmoid(acc[:, OFF_GLU_G:OFF_GLU_G + CONV_D])
        for cb in range(CONV_LB):
            vpad_ref[cb, CONV_HALO:n, :] = v[:, cb * LANE:(cb + 1) * LANE]

    def shift_task(cb):
        def run():
            vp = vpad_ref[cb, 0:n, :]
            for r in range(1, SUBLANE):
                sh_ref[r - 1, 0:n, :] = pltpu.roll(vp, n - r, axis=0)
        return run

    def tap_task(cb, r0):
        def run():
            cacc = jnp.broadcast_to(cbias_ref[cb], (CONV_RB, LANE))
            for tap in range(CONV_K):
                q, r = divmod(first + tap, SUBLANE)
                start = r0 + q * SUBLANE
                win = (vpad_ref[cb, start:start + CONV_RB, :] if r == 0
                       else sh_ref[r - 1, start:start + CONV_RB, :])
                cacc = cacc + cw_ref[cb, tap:tap + 1, :] * win
            cacc_ref[cb, r0:r0 + CONV_RB, :] = cacc
        return run

    @pl.when(j > 0)
    def _():
        tasks = []
        for c in range(CONV_LB_PER_STEP):
            cb = jnp.minimum((j - 1) * CONV_LB_PER_STEP + c, CONV_LB - 1)
            tasks.append(shift_task(cb))
            tasks.extend(tap_task(cb, r0) for r0 in range(0, tm, CONV_RB))
        tn = o_ref.shape[1]
        n_chunks = tn // PROJ_CHUNK
        per_chunk = -(-len(tasks) // n_chunks)
        u = u_ref[...]
        for k in range(n_chunks):
            cols = slice(k * PROJ_CHUNK, (k + 1) * PROJ_CHUNK)
            acc = _dot(u, w_ref[:, cols])
            o_ref[:, cols] = acc.astype(o_ref.dtype)
            for task in tasks[k * per_chunk:(k + 1) * per_chunk]:
                task()

        @pl.when(j == nj - 1)
        def _():
            tail_ref[...] = acc

    @pl.when(j == nj - 1)
    def _():
        v = jnp.concatenate([cacc_ref[cb] for cb in range(CONV_LB)], axis=-1)
        mu = jnp.mean(v, axis=-1, keepdims=True)
        vc = v - mu
        var = jnp.mean(vc * vc, axis=-1, keepdims=True)
        y = vc * lax.rsqrt(var + EPS) * lng_ref[...] + lnb_ref[...]
        conv_ref[...] = _silu(y).astype(conv_ref.dtype)


def _in_proj_conv(x2, g, w, cw, cbias, lng, lnb, tm, tn, blocks_per_seq):
    t, d = x2.shape
    n = w.shape[1]
    assert (n // tn - 1) * CONV_LB_PER_STEP >= CONV_LB and OFF_GLU_G + CONV_D <= tn
    full = lambda shape: pl.BlockSpec(shape, lambda i, j: (0,) * len(shape))
    rows = CONV_HALO + tm + SUBLANE
    return pl.pallas_call(
        functools.partial(_in_proj_conv_kernel, tm=tm, blocks_per_seq=blocks_per_seq),
        grid=(t // tm, n // tn),
        in_specs=[
            pl.BlockSpec((tm, d), lambda i, j: (i, 0)),
            full((1, d)),
            pl.BlockSpec((d, tn), lambda i, j: (0, j)),
            full((CONV_LB, CONV_K, LANE)), full((CONV_LB, 1, LANE)), full((1, CONV_D)), full((1, CONV_D)),
        ],
        out_specs=[pl.BlockSpec((tm, tn), lambda i, j: (i, j)), pl.BlockSpec((tm, TAIL_W), lambda i, j: (i, 0)),
                   pl.BlockSpec((tm, CONV_D), lambda i, j: (i, 0))],
        out_shape=[jax.ShapeDtypeStruct((t, n), BF16), jax.ShapeDtypeStruct((t, TAIL_W), F32),
                   jax.ShapeDtypeStruct((t, CONV_D), BF16)],
        scratch_shapes=[pltpu.VMEM((tm, d), BF16),
                        pltpu.VMEM((CONV_LB, rows, LANE), F32),
                        pltpu.VMEM((SUBLANE - 1, rows, LANE), F32),
                        pltpu.VMEM((CONV_LB, tm, LANE), F32)],
        compiler_params=_cparams(("arbitrary", "arbitrary")),
        name="in_proj_conv",
    )(x2, g, w, cw, cbias, lng, lnb)


def _ssd_kernel(xbc_ref, z_ref, misc_ref, convw_ref, convb_ref, dtb_ref, alog_s_ref,
                dskip_ref, normg_ref, expand_ref, tri_ref, shift_ref, y_ref, xprev_ref, state_ref, *, L, n_chunks):
    ts = L * n_chunks

    @pl.when(pl.program_id(1) == 0)
    def _():
        xprev_ref[...] = jnp.zeros(xprev_ref.shape, xprev_ref.dtype)
        state_ref[...] = jnp.zeros(state_ref.shape, F32)

    xb = xbc_ref[0]
    xe = jnp.concatenate([xprev_ref[...], xb], axis=0)
    xprev_ref[...] = xb[ts - SSD_HALO:ts]
    parts = []
    for c in range(n_chunks):
        sh = _dot(shift_ref[...], xe[c * L:c * L + L + SSD_HALO])
        part = convb_ref[...] + convw_ref[SSD_CONV - 1:SSD_CONV, :] * xb[c * L:(c + 1) * L].astype(F32)
        for d in range(1, SSD_CONV):
            part = part + convw_ref[SSD_CONV - 1 - d:SSD_CONV - d, :] * sh[(d - 1) * L:d * L]
        parts.append(part)
    xc = _silu(jnp.concatenate(parts, axis=0))

    misc = misc_ref[0]
    v = misc + dtb_ref[...]
    dt_s = jnp.maximum(v, 0.0) + jnp.log1p(jnp.exp(-jnp.abs(v)))
    a_neg_s = -jnp.exp(alog_s_ref[...])
    expand = expand_ref[...]
    tri = tri_ref[...]
    dt_e = _dot_exact_rhs01(dt_s, expand)
    a_s = dt_s * a_neg_s

    row_i = lax.broadcasted_iota(jnp.int32, (L, L), 0)
    col_i = lax.broadcasted_iota(jnp.int32, (L, L), 1)
    causal = row_i >= col_i
    lane_g = lax.broadcasted_iota(jnp.int32, (1, SSD_GROUP_W), 1) // SSD_HEAD_DIM

    z = z_ref[0].astype(F32)
    for c in range(n_chunks):
        r0 = c * L
        xs = xc[r0:r0 + L, 0:SSD_D_INNER]
        acs_s = _dot_exact_lhs01(tri, a_s[r0:r0 + L])
        acs_e = _dot_exact_rhs01(acs_s, expand)
        acs_st = acs_s.T
        last_e = acs_e[L - 1:L, :]
        exp_acs = jnp.exp(acs_e)
        to_end = jnp.exp(last_e - acs_e)
        chunk_decay = jnp.exp(last_e)
        xdt = xs * dt_e[r0:r0 + L]
        xdt_b = xdt.astype(BF16)
        xend_b = (xdt * to_end).astype(BF16)
        y_groups = []
        for g in range(SSD_N_GROUPS):
            gs = slice(g * SSD_GROUP_W, (g + 1) * SSD_GROUP_W)
            b_g = xc[r0:r0 + L, SSD_D_INNER + g * SSD_D_STATE:SSD_D_INNER + (g + 1) * SSD_D_STATE]
            c_g = xc[r0:r0 + L, SSD_D_INNER + (SSD_N_GROUPS + g) * SSD_D_STATE:
                     SSD_D_INNER + (SSD_N_GROUPS + g + 1) * SSD_D_STATE]
            b_gt = b_g.T.astype(BF16)
            c_gb = c_g.astype(BF16)
            cb = _dot(c_gb, b_gt)
            st = state_ref[g]
            y_g = _dot(c_gb, st.astype(BF16)) * exp_acs[:, gs]
            x_g = xdt_b[:, gs]
            for r in range(SSD_HEADS_PER_GROUP):
                h = g * SSD_HEADS_PER_GROUP + r
                col = acs_s[:, MISC_DT + h:MISC_DT + h + 1]
                row = acs_st[MISC_DT + h:MISC_DT + h + 1, :]
                decay = jnp.exp(jnp.where(causal, col - row, -jnp.inf))
                gm = (cb * decay).astype(BF16)
                x_h = jnp.where(lane_g == r, x_g, jnp.zeros_like(x_g))
                y_g = y_g + _dot(gm, x_h)
            state_ref[g] = st * chunk_decay[:, gs] + _dot(b_gt, xend_b[:, gs])
            y_groups.append(y_g)
        y = jnp.concatenate(y_groups, axis=-1) + dskip_ref[...] * xs
        y = y * _silu(z[r0:r0 + L])
        outs = []
        for g in range(SSD_N_GROUPS):
            yg = y[:, g * SSD_GROUP_W:(g + 1) * SSD_GROUP_W]
            outs.append(yg * lax.rsqrt(jnp.mean(yg * yg, axis=-1, keepdims=True) + EPS))
        y = jnp.concatenate(outs, axis=-1) * normg_ref[...]
        y_ref[0, r0:r0 + L, :] = y.astype(y_ref.dtype)


def _ssd(proj3, tail3, convw, convb, dtb, alog_s, dskip_e, normg, expand, tri, shift, L, n_chunks):
    b, s, _ = proj3.shape
    ts = L * n_chunks
    full = lambda shape: pl.BlockSpec(shape, lambda bi, si: (0,) * len(shape))
    return pl.pallas_call(
        functools.partial(_ssd_kernel, L=L, n_chunks=n_chunks),
        grid=(b, s // ts),
        in_specs=[
            pl.BlockSpec((1, ts, SSD_XBC), lambda bi, si: (bi, si, OFF_XBC // SSD_XBC)),
            pl.BlockSpec((1, ts, SSD_D_INNER), lambda bi, si: (bi, si, OFF_Z // SSD_D_INNER)),
            pl.BlockSpec((1, ts, LANE), lambda bi, si: (bi, si, TAIL_MISC)),
            full((SSD_CONV, SSD_XBC)), full((1, SSD_XBC)), full((1, LANE)), full((1, LANE)),
            full((1, SSD_D_INNER)), full((1, SSD_D_INNER)),
            full((LANE, SSD_D_INNER)), full((L, L)), full(shift.shape),
        ],
        out_specs=pl.BlockSpec((1, ts, SSD_D_INNER), lambda bi, si: (bi, si, 0)),
        out_shape=jax.ShapeDtypeStruct((b, s, SSD_D_INNER), BF16),
        scratch_shapes=[
            pltpu.VMEM((SSD_HALO, SSD_XBC), BF16),
            pltpu.VMEM((SSD_N_GROUPS, SSD_D_STATE, SSD_GROUP_W), F32),
        ],
        compiler_params=_cparams(("parallel", "arbitrary")),
        name="ssd",
    )(proj3, proj3, tail3, convw, convb, dtb, alog_s, dskip_e, normg, expand, tri, shift)


def _mla_prep_kernel(qlat_ref, ckv_ref, misc_ref, krot_ref, cos_ref, sin_ref, qag_ref, wq_ref, kvag_ref,
                     wkv_ref, qg_nope_ref, qg_rope_ref, qg_rot_ref, kg_nope_ref, kg_rope_ref, kg_rot_ref,
                     q_ref, k_ref, v_ref):
    hn = MLA_HEADS * MLA_NOPE
    cos2 = cos_ref[0]
    sin2 = sin_ref[0]
    scale = MLA_QK ** -0.5 * math.log2(math.e)

    ql = qlat_ref[0].astype(F32)
    ms = jnp.sum(ql * ql, axis=-1, keepdims=True) * (1.0 / MLA_Q_RANK)
    qa = (ql * lax.rsqrt(ms + EPS) * qag_ref[...]).astype(BF16)
    qf = _dot(qa, wq_ref[...])
    for h in range(MLA_HEADS):
        qn = qf[:, h * LANE:(h + 1) * LANE]
        qn = qn * lax.rsqrt(jnp.mean(qn * qn, axis=-1, keepdims=True) + EPS) * qg_nope_ref[...]
        qr = qf[:, hn + h * LANE:hn + (h + 1) * LANE]
        qs = qf[:, 2 * hn + h * LANE:2 * hn + (h + 1) * LANE]
        inv = lax.rsqrt(jnp.sum(qr * qr, axis=-1, keepdims=True) * (1.0 / MLA_ROPE) + EPS)
        qro = (qr * qg_rope_ref[...] * cos2 + qs * qg_rot_ref[...] * sin2) * inv
        q_ref[0, :, h * MLA_QK_PAD:h * MLA_QK_PAD + LANE] = (qn * scale).astype(q_ref.dtype)
        q_ref[0, :, h * MLA_QK_PAD + LANE:(h + 1) * MLA_QK_PAD] = (qro * scale).astype(q_ref.dtype)

    ckv = ckv_ref[0].astype(F32)
    ca = (ckv * lax.rsqrt(jnp.mean(ckv * ckv, axis=-1, keepdims=True) + EPS) * kvag_ref[...]).astype(BF16)
    kvf = _dot(ca, wkv_ref[...])
    lane = lax.broadcasted_iota(jnp.int32, (1, LANE), 1)
    kr = jnp.where(lane < MLA_ROPE, misc_ref[0], 0.0)
    ks = krot_ref[0]
    inv = lax.rsqrt(jnp.sum(kr * kr, axis=-1, keepdims=True) * (1.0 / MLA_ROPE) + EPS)
    kro_t = ((kr * kg_rope_ref[...] * cos2 + ks * kg_rot_ref[...] * sin2) * inv).T.astype(k_ref.dtype)
    for h in range(MLA_HEADS):
        kn = kvf[:, h * LANE:(h + 1) * LANE]
        kn = kn * lax.rsqrt(jnp.mean(kn * kn, axis=-1, keepdims=True) + EPS) * kg_nope_ref[...]
        k_ref[0, h, 0, 0:LANE, :] = kn.T.astype(k_ref.dtype)
        k_ref[0, h, 0, LANE:MLA_QK_PAD, :] = kro_t
    v_ref[0] = kvf[:, hn:].astype(v_ref.dtype)


def _mla_prep(proj3, tail3, cos2, sin2, qag, wq, kvag, wkv, qg_nope, qg_rope, qg_rot, kg_nope, kg_rope, kg_rot, tm):
    b, s, _ = proj3.shape
    full = lambda shape: pl.BlockSpec(shape, lambda bi, si: (0,) * len(shape))
    hq = MLA_HEADS * MLA_QK_PAD
    hv = MLA_HEADS * MLA_V
    return pl.pallas_call(
        _mla_prep_kernel,
        grid=(b, s // tm),
        in_specs=[
            pl.BlockSpec((1, tm, MLA_Q_RANK_PAD), lambda bi, si: (bi, si, OFF_QLAT // MLA_Q_RANK_PAD)),
            pl.BlockSpec((1, tm, MLA_KV_RANK), lambda bi, si: (bi, si, OFF_CKV // MLA_KV_RANK)),
            pl.BlockSpec((1, tm, LANE), lambda bi, si: (bi, si, TAIL_MISC)),
            pl.BlockSpec((1, tm, LANE), lambda bi, si: (bi, si, TAIL_KROT)),
            pl.BlockSpec((1, tm, LANE), lambda bi, si: (bi, si, 0)),
            pl.BlockSpec((1, tm, LANE), lambda bi, si: (bi, si, 0)),
            full((1, MLA_Q_RANK_PAD)), full(wq.shape), full((1, MLA_KV_RANK)), full(wkv.shape),
            full((1, LANE)), full((1, LANE)), full((1, LANE)), full((1, LANE)), full((1, LANE)), full((1, LANE)),
        ],
        out_specs=[
            pl.BlockSpec((1, tm, hq), lambda bi, si: (bi, si, 0)),
            pl.BlockSpec((1, MLA_HEADS, 1, MLA_QK_PAD, tm), lambda bi, si: (bi, 0, si, 0, 0)),
            pl.BlockSpec((1, tm, hv), lambda bi, si: (bi, si, 0)),
        ],
        out_shape=[
            jax.ShapeDtypeStruct((b, s, hq), BF16),
            jax.ShapeDtypeStruct((b, MLA_HEADS, s // tm, MLA_QK_PAD, tm), BF16),
            jax.ShapeDtypeStruct((b, s, hv), BF16),
        ],
        compiler_params=_cparams(("parallel", "parallel")),
        name="mla_prep",
    )(proj3, proj3, tail3, tail3, cos2, sin2, qag, wq, kvag, wkv, qg_nope, qg_rope, qg_rot, kg_nope, kg_rope,
      kg_rot)


def _attn_kernel(q_ref, kt_ref, v_ref, o_ref, sa_ref, sb_ref, m_ref, acc_ref, *, tq):
    i = pl.program_id(2)
    q = q_ref[0]
    reps = tq // LANE

    def scores(j, s_ref):
        s_ref[...] = _dot(q, kt_ref[0, 0, j])

    def consume(j, s_ref, masked):
        start = pl.multiple_of(j * tq, tq)
        v = v_ref[0, pl.ds(start, tq), :]
        v1 = jnp.concatenate([v, jnp.ones_like(v)], axis=-1)
        for r0 in range(0, tq, tq // ATTN_ROW_SPLIT):
            rows = slice(r0, r0 + tq // ATTN_ROW_SPLIT)
            s = s_ref[rows, :]
            if masked:
                rq = (lax.broadcasted_iota(jnp.int32, (tq // ATTN_ROW_SPLIT, tq), 0) + r0) // CHUNK
                ck = lax.broadcasted_iota(jnp.int32, (tq // ATTN_ROW_SPLIT, tq), 1) // CHUNK
                s = jnp.where(ck <= rq, s, -jnp.inf)
            m = m_ref[rows, :]
            m_new = jnp.maximum(m, jnp.max(s, axis=-1, keepdims=True))
            p = jnp.exp2(s - jnp.concatenate([m_new] * reps, axis=-1))
            alpha = jnp.exp2(m - m_new)
            acc_ref[rows, :] = (jnp.concatenate([alpha, alpha], axis=-1) * acc_ref[rows, :]
                                + _dot(p.astype(BF16), v1))
            m_ref[rows, :] = m_new

    m_ref[...] = jnp.full(m_ref.shape, -jnp.inf, F32)
    acc_ref[...] = jnp.zeros(acc_ref.shape, F32)
    scores(0, sa_ref)

    def pair(jj, carry):
        j = 2 * jj
        scores(j + 1, sb_ref)
        consume(j, sa_ref, False)
        scores(j + 2, sa_ref)
        consume(j + 1, sb_ref, False)
        return carry

    lax.fori_loop(0, i // 2, pair, 0)

    @pl.when(i % 2 == 0)
    def _():
        consume(i, sa_ref, True)

    @pl.when(i % 2 == 1)
    def _():
        scores(i, sb_ref)
        consume(i - 1, sa_ref, False)
        consume(i, sb_ref, True)

    acc = acc_ref[...]
    o_ref[0] = (acc[:, :MLA_V] / acc[:, MLA_V:]).astype(o_ref.dtype)


def _attention(q, kt, v, tq):
    b, s, _ = q.shape
    n = s // tq
    return pl.pallas_call(
        functools.partial(_attn_kernel, tq=tq),
        grid=(b, MLA_HEADS, n),
        in_specs=[
            pl.BlockSpec((1, tq, MLA_QK_PAD), lambda bi, h, i: (bi, i, h)),
            pl.BlockSpec((1, 1, n, MLA_QK_PAD, tq), lambda bi, h, i: (bi, h, 0, 0, 0)),
            pl.BlockSpec((1, s, MLA_V), lambda bi, h, i: (bi, 0, h)),
        ],
        out_specs=pl.BlockSpec((1, tq, MLA_V), lambda bi, h, i: (bi, i, h)),
        out_shape=jax.ShapeDtypeStruct((b, s, MLA_HEADS * MLA_V), BF16),
        scratch_shapes=[pltpu.VMEM((tq, tq), F32), pltpu.VMEM((tq, tq), F32), pltpu.VMEM((tq, LANE), F32),
                        pltpu.VMEM((tq, 2 * MLA_V), F32)],
        compiler_params=_cparams(("parallel", "parallel", "arbitrary")),
        name="mla_attention",
    )(q, kt, v)


def _merge_kernel(x_ref, ssd_ref, conv_ref, att_ref, g0_ref, g1_ref, g2_ref, gb_ref, w_ssd_ref, w_conv_ref,
                  w_mla_ref, w_out_ref, o_ref):
    merged = _sigmoid(g0_ref[...].astype(F32) + gb_ref[0:1, :]) * _dot(ssd_ref[...], w_ssd_ref[...])
    merged = merged + _sigmoid(g1_ref[...].astype(F32) + gb_ref[1:2, :]) * _dot(conv_ref[...], w_conv_ref[...])
    merged = merged + _sigmoid(g2_ref[...].astype(F32) + gb_ref[2:3, :]) * _dot(att_ref[...], w_mla_ref[...])
    o_ref[...] = x_ref[...] + _dot(merged.astype(BF16), w_out_ref[...])


def _merge(x2, ssd_pre, conv_pre, att, proj2, gate_b, w_ssd, w_conv, w_mla, w_out, tm):
    t, d = x2.shape
    row = lambda c: pl.BlockSpec((tm, d), lambda i, c=c: (i, c))
    full = lambda shape: pl.BlockSpec(shape, lambda i: (0,) * len(shape))
    g0 = OFF_GATE // d
    return pl.pallas_call(
        _merge_kernel,
        grid=(t // tm,),
        in_specs=[row(0), row(0), row(0), row(0), row(g0), row(g0 + 1), row(g0 + 2), full(gate_b.shape),
                  full((d, d)), full((d, d)), full((d, d)), full((d, d))],
        out_specs=row(0),
        out_shape=jax.ShapeDtypeStruct((t, d), F32),
        compiler_params=_cparams(("parallel",)),
        name="merge",
    )(x2, ssd_pre, conv_pre, att, proj2, proj2, proj2, gate_b, w_ssd, w_conv, w_mla, w_out)


def _mem_kv_kernel(mem_ref, g_ref, w_ref, kg_ref, k_ref, v_ref):
    m = mem_ref[0]
    mn = (m * lax.rsqrt(jnp.mean(m * m, axis=-1, keepdims=True) + EPS) * g_ref[...]).astype(BF16)
    kv = _dot(mn, w_ref[...])
    for h in range(X_HEADS):
        kh = kv[:, h * X_HEAD_DIM:(h + 1) * X_HEAD_DIM]
        kh = kh * lax.rsqrt(jnp.mean(kh * kh, axis=-1, keepdims=True) + EPS) * kg_ref[...]
        k_ref[0, :, h * X_HEAD_DIM:(h + 1) * X_HEAD_DIM] = kh.astype(k_ref.dtype)
    v_ref[0] = kv[:, D_MODEL:].astype(v_ref.dtype)


def _mem_kv(mem, g, w_kv, kg):
    b, m, d = mem.shape
    full = lambda shape: pl.BlockSpec(shape, lambda bi: (0,) * len(shape))
    blk = pl.BlockSpec((1, m, d), lambda bi: (bi, 0, 0))
    return pl.pallas_call(
        _mem_kv_kernel,
        grid=(b,),
        in_specs=[blk, full((1, d)), full(w_kv.shape), full((1, X_HEAD_DIM))],
        out_specs=[blk, blk],
        out_shape=[jax.ShapeDtypeStruct((b, m, d), BF16), jax.ShapeDtypeStruct((b, m, d), BF16)],
        compiler_params=_cparams(("parallel",)),
        name="mem_kv",
    )(mem, g, w_kv, kg)


def _xattn_kernel(x_ref, k_ref, v_ref, g_ref, wq_ref, qg_ref, wo_ref, o_ref):
    x = x_ref[0]
    u = (x * lax.rsqrt(jnp.mean(x * x, axis=-1, keepdims=True) + EPS) * g_ref[...]).astype(BF16)
    q = _dot(u, wq_ref[...])
    scale = X_HEAD_DIM ** -0.5
    outs = []
    for h in range(X_HEADS):
        hs = slice(h * X_HEAD_DIM, (h + 1) * X_HEAD_DIM)
        qh = q[:, hs]
        qh = qh * lax.rsqrt(jnp.mean(qh * qh, axis=-1, keepdims=True) + EPS) * (qg_ref[...] * scale)
        s = _dot_nt(qh.astype(BF16), k_ref[0, :, hs])
        p = jnp.exp(s - jnp.max(s, axis=-1, keepdims=True))
        o = _dot(p.astype(BF16), v_ref[0, :, hs])
        outs.append((o / jnp.sum(p, axis=-1, keepdims=True)).astype(BF16))
    o_ref[0] = x + _dot(jnp.concatenate(outs, axis=-1), wo_ref[...])


def _xattn(x3, k, v, g, wq, qg, wo, tm):
    b, s, d = x3.shape
    m = k.shape[1]
    full = lambda shape: pl.BlockSpec(shape, lambda bi, si: (0,) * len(shape))
    return pl.pallas_call(
        _xattn_kernel,
        grid=(b, s // tm),
        in_specs=[
            pl.BlockSpec((1, tm, d), lambda bi, si: (bi, si, 0)),
            pl.BlockSpec((1, m, d), lambda bi, si: (bi, 0, 0)),
            pl.BlockSpec((1, m, d), lambda bi, si: (bi, 0, 0)),
            full((1, d)), full((d, d)), full((1, X_HEAD_DIM)), full((d, d)),
        ],
        out_specs=pl.BlockSpec((1, tm, d), lambda bi, si: (bi, si, 0)),
        out_shape=jax.ShapeDtypeStruct((b, s, d), F32),
        compiler_params=_cparams(("parallel", "parallel")),
        name="mem_xattn",
    )(x3, k, v, g, wq, qg, wo)


def _ffn_kernel(x_ref, g_ref, wg_ref, wu_ref, wo_ref, o_ref, *, hc):
    x = x_ref[...]
    u = (x * lax.rsqrt(jnp.mean(x * x, axis=-1, keepdims=True) + EPS) * g_ref[...]).astype(BF16)
    acc = x
    for c in range(0, FFN_HIDDEN, hc):
        gate = _dot(u, wg_ref[:, c:c + hc])
        up = _dot(u, wu_ref[:, c:c + hc])
        acc = acc + _dot((_silu(gate) * up).astype(BF16), wo_ref[c:c + hc, :])
    o_ref[...] = acc


def _ffn(x2, g, wg, wu, wo, tm, hc):
    t, d = x2.shape
    full = lambda shape: pl.BlockSpec(shape, lambda i: (0,) * len(shape))
    return pl.pallas_call(
        functools.partial(_ffn_kernel, hc=hc),
        grid=(t // tm,),
        in_specs=[pl.BlockSpec((tm, d), lambda i: (i, 0)), full((1, d)), full(wg.shape), full(wu.shape),
                  full(wo.shape)],
        out_specs=pl.BlockSpec((tm, d), lambda i: (i, 0)),
        out_shape=jax.ShapeDtypeStruct((t, d), F32),
        compiler_params=_cparams(("parallel",)),
        name="ffn",
    )(x2, g, wg, wu, wo)


def _swap_halves(a):
    h = a.shape[-1] // 2
    return jnp.concatenate([a[..., h:], a[..., :h]], axis=-1)


def _pad_last(a, width):
    return jnp.pad(a, [(0, 0)] * (a.ndim - 1) + [(0, width - a.shape[-1])])


def _in_proj_weight(w):
    z, xbc, dt, glu, q_lat, kv_lat, gates = jnp.split(
        w, np.cumsum([SSD_D_INNER, SSD_XBC, SSD_N_HEADS, 2 * CONV_D, MLA_Q_RANK, MLA_KV_RANK + MLA_ROPE])
        .tolist(), axis=-1)
    c_kv, k_rope = kv_lat[:, :MLA_KV_RANK], kv_lat[:, MLA_KV_RANK:]
    misc = _pad_last(jnp.concatenate([k_rope, dt], axis=-1), LANE)
    krot = _pad_last(_swap_halves(k_rope), LANE)
    out = jnp.concatenate([glu, xbc, gates, z, _pad_last(q_lat, MLA_Q_RANK_PAD), c_kv, misc, krot], axis=-1)
    return _pad_last(out, PROJ_W).astype(BF16)


def _mla_q_weight(w):
    w = w.reshape(MLA_Q_RANK, MLA_HEADS, MLA_QK)
    nope = w[:, :, :MLA_NOPE].reshape(MLA_Q_RANK, -1)
    rope = w[:, :, MLA_NOPE:]
    rope_p = _pad_last(rope, LANE).reshape(MLA_Q_RANK, -1)
    rot_p = _pad_last(_swap_halves(rope), LANE).reshape(MLA_Q_RANK, -1)
    out = jnp.concatenate([nope, rope_p, rot_p], axis=-1)
    return jnp.pad(out, [(0, MLA_Q_RANK_PAD - MLA_Q_RANK), (0, 0)]).astype(BF16)


def _mla_kv_weight(w):
    w = w.reshape(MLA_KV_RANK, MLA_HEADS, MLA_NOPE + MLA_V)
    return jnp.concatenate([w[:, :, :MLA_NOPE].reshape(MLA_KV_RANK, -1),
                            w[:, :, MLA_NOPE:].reshape(MLA_KV_RANK, -1)], axis=-1).astype(BF16)


def _row(a, width=None):
    a = a.reshape(1, -1).astype(F32)
    return a if width is None else _pad_last(a, width)


def _pick(n, prefs):
    for p in prefs:
        if n % p == 0:
            return p
    return n


def kernel(x, mem, positions, mix_norm_g, w_in, ssd_conv_w, ssd_conv_b, ssd_dt_bias, ssd_a_log, ssd_d, ssd_norm_g, ssd_w_out, conv_dw_w, conv_dw_b, conv_ln_g, conv_ln_b, conv_w_out, mla_q_a_g, mla_w_q_b, mla_kv_a_g, mla_w_kv_b, mla_q_norm_g, mla_k_norm_g, mla_w_o, gate_b, w_out, xattn_norm_g, mem_norm_g, xattn_w_q, xattn_w_kv, xattn_q_norm_g, xattn_k_norm_g, xattn_w_o, ffn_norm_g, ffn_w_in, ffn_w_out):
    b, s, d = x.shape
    t = b * s
    depth = w_in.shape[0]

    inv = ROPE_THETA ** (-jnp.arange(0, MLA_ROPE, 2, dtype=F32) / MLA_ROPE)
    ang = positions.astype(F32)[..., None] * inv
    cos, sin = jnp.cos(ang), jnp.sin(ang)
    cos2 = _pad_last(jnp.concatenate([cos, cos], axis=-1), LANE)
    sin2 = _pad_last(jnp.concatenate([-sin, sin], axis=-1), LANE)

    ssd_l = _pick(s, (128, 64))
    ssd_nc = _pick(s // ssd_l, (4, 2, 1))
    head_of_lane = np.arange(SSD_D_INNER) // SSD_HEAD_DIM
    expand = jnp.asarray((np.arange(LANE)[:, None] - MISC_DT) == head_of_lane[None, :], dtype=BF16)
    tri = jnp.asarray(np.tril(np.ones((ssd_l, ssd_l), np.float32)), dtype=BF16)
    shift_np = np.zeros((SSD_CONV - 1, ssd_l, ssd_l + SSD_HALO), np.float32)
    for dd in range(1, SSD_CONV):
        shift_np[dd - 1, np.arange(ssd_l), SSD_HALO + np.arange(ssd_l) - dd] = 1.0
    shift = jnp.asarray(shift_np.reshape((SSD_CONV - 1) * ssd_l, ssd_l + SSD_HALO), dtype=BF16)

    tm_proj = _pick(s, (1024, 512, 256))
    tm_tok = _pick(t, (512, 256))
    tq = _pick(s, (512, 256, 128))

    x2 = x.reshape(t, d)
    for l in range(depth):
        conv_w_lb = conv_dw_w[l].astype(F32).reshape(CONV_K, CONV_LB, LANE).transpose(1, 0, 2)
        proj, tail, conv_pre = _in_proj_conv(
            x2, _row(mix_norm_g[l]), _in_proj_weight(w_in[l]), conv_w_lb,
            conv_dw_b[l].astype(F32).reshape(CONV_LB, 1, LANE), _row(conv_ln_g[l]), _row(conv_ln_b[l]),
            tm_proj, 2304, s // tm_proj)
        proj3 = proj.reshape(b, s, PROJ_W)
        tail3 = tail.reshape(b, s, TAIL_W)

        dtb = jnp.pad(_row(ssd_dt_bias[l]), [(0, 0), (MISC_DT, LANE - MISC_DT - SSD_N_HEADS)])
        alog_s = jnp.pad(_row(ssd_a_log[l]), [(0, 0), (MISC_DT, LANE - MISC_DT - SSD_N_HEADS)])
        dskip_e = jnp.repeat(_row(ssd_d[l]), SSD_HEAD_DIM, axis=-1)
        ssd_pre = _ssd(proj3, tail3, ssd_conv_w[l], _row(ssd_conv_b[l]), dtb, alog_s, dskip_e,
                       _row(ssd_norm_g[l]), expand, tri, shift, ssd_l, ssd_nc)

        qg, kg = mla_q_norm_g[l], mla_k_norm_g[l]
        q, k, v = _mla_prep(
            proj3, tail3, cos2, sin2, _row(mla_q_a_g[l], MLA_Q_RANK_PAD), _mla_q_weight(mla_w_q_b[l]),
            _row(mla_kv_a_g[l]), _mla_kv_weight(mla_w_kv_b[l]),
            _row(qg[:MLA_NOPE]), _row(qg[MLA_NOPE:], LANE), _row(_swap_halves(qg[MLA_NOPE:]), LANE),
            _row(kg[:MLA_NOPE]), _row(kg[MLA_NOPE:], LANE), _row(_swap_halves(kg[MLA_NOPE:]), LANE), tq)
        att = _attention(q, k, v, tq)

        x2 = _merge(x2, ssd_pre.reshape(t, d), conv_pre.reshape(t, d), att.reshape(t, d), proj,
                    gate_b[l].astype(F32), ssd_w_out[l].astype(BF16), conv_w_out[l].astype(BF16),
                    mla_w_o[l].astype(BF16), w_out[l].astype(BF16), tm_tok)

        mk, mv = _mem_kv(mem, _row(mem_norm_g[l]), xattn_w_kv[l].astype(BF16), _row(xattn_k_norm_g[l]))
        x2 = _xattn(x2.reshape(b, s, d), mk, mv, _row(xattn_norm_g[l]), xattn_w_q[l].astype(BF16),
                    _row(xattn_q_norm_g[l]), xattn_w_o[l].astype(BF16), tm_tok).reshape(t, d)

        w_ffn = ffn_w_in[l].astype(BF16)
        x2 = _ffn(x2, _row(ffn_norm_g[l]), w_ffn[:, :FFN_HIDDEN], w_ffn[:, FFN_HIDDEN:],
                  ffn_w_out[l].astype(BF16), tm_tok, 256)
    return x2.reshape(b, s, d)
```
